```python
import math
import jax
import jax.numpy as jnp
from jax import lax
import numpy as np

D_MODEL = 1024
BATCH = 8
SEQ = 4096
DEPTH = 4

CTX_LEN = 256
GRID_W = 64
EPS = 1e-6
ROPE_BASE = 10000.0
CHUNK = 128
Q_BLOCK = 128
CONV_W = 5

MLA_HEADS = 8
MLA_NOPE = 64
MLA_ROPE = 32
MLA_V = 64
MLA_Q_LORA = 384
MLA_KV_LORA = 256
MLA_SCALE = (MLA_NOPE + MLA_ROPE) ** -0.5
MLA_OUT_W = MLA_HEADS * MLA_V

ML_HEADS = 4
ML_QK = 64
ML_V = 128
ML_QK_W = ML_HEADS * ML_QK
ML_V_W = ML_HEADS * ML_V

SSD_HEADS = 16
SSD_P = 64
SSD_N = 128
SSD_GROUPS = 4
SSD_HPG = SSD_HEADS // SSD_GROUPS
SSD_INNER = SSD_HEADS * SSD_P
SSD_BC_W = SSD_GROUPS * SSD_N

N_BRANCH = 3

N_EXPERTS = 32
TOP_K = 4
D_FF = 1024
SWIGLU_LIMIT = 7.0
SWIGLU_ALPHA = 1.702
MOE_BLOCK = 128

IN_SIZES = (MLA_Q_LORA, MLA_KV_LORA, MLA_ROPE,
            2 * ML_QK_W, ML_V_W, ML_V_W, 2 * 2 * ML_HEADS,
            SSD_INNER, SSD_INNER + 2 * SSD_BC_W, 2 * SSD_HEADS,
            N_BRANCH * D_MODEL)
D_IN = sum(IN_SIZES)

kernel_name = 'hybrid_mla_mlstm_ssd_moe_dit'


def rmsnorm(x, g):
    xf = x.astype(jnp.float32)
    y = xf * lax.rsqrt(jnp.mean(xf * xf, axis=-1, keepdims=True) + EPS)
    return (y * g.astype(jnp.float32)).astype(x.dtype)


def modulate(x, g, shift, scale):
    return rmsnorm(x, g) * (1 + scale) + shift


def split_cols(u):
    idx = [int(s) for s in np.cumsum(IN_SIZES)[:-1]]
    return jnp.split(u, idx, axis=-1)


def axial_rope(n_tok):
    rows = n_tok // GRID_W
    row = jnp.broadcast_to(jnp.arange(rows)[:, None], (rows, GRID_W)).reshape(-1)
    col = jnp.broadcast_to(jnp.arange(GRID_W)[None, :], (rows, GRID_W)).reshape(-1)
    n_freq = MLA_ROPE // 4
    inv = ROPE_BASE ** (-jnp.arange(n_freq, dtype=jnp.float32) / n_freq)
    ang = jnp.concatenate([row[:, None] * inv, col[:, None] * inv], axis=-1)
    return jnp.cos(ang), jnp.sin(ang)


def apply_rope(x, cos, sin):
    xf = x.astype(jnp.float32)
    x1, x2 = jnp.split(xf, 2, axis=-1)
    cs, sn = cos[:, None, :], sin[:, None, :]
    return jnp.concatenate([x1 * cs - x2 * sn, x2 * cs + x1 * sn], axis=-1).astype(x.dtype)


def dwconv_centred(x, w, b):
    pad = CONV_W // 2
    y = lax.conv_general_dilated(x, w[:, None, :].astype(x.dtype), window_strides=(1,),
                                 padding=((pad, pad),), dimension_numbers=('NWC', 'WIO', 'NWC'),
                                 feature_group_count=x.shape[-1])
    return y + b.astype(x.dtype)


def mla_queries(u_q, qnorm_g, w_uq, rope):
    b, t, _ = u_q.shape
    q = (rmsnorm(u_q, qnorm_g) @ w_uq).reshape(b, t, MLA_HEADS, MLA_NOPE + MLA_ROPE)
    if rope is not None:
        q = jnp.concatenate([q[..., :MLA_NOPE], apply_rope(q[..., MLA_NOPE:], *rope)], axis=-1)
    return q


def mla_keys_values(u_kv, u_kr, kvnorm_g, w_ukv, rope):
    b, t, _ = u_kv.shape
    kv = (rmsnorm(u_kv, kvnorm_g) @ w_ukv).reshape(b, t, MLA_HEADS, MLA_NOPE + MLA_V)
    k_rope = u_kr[:, :, None, :]
    if rope is not None:
        k_rope = apply_rope(k_rope, *rope)
    k = jnp.concatenate([kv[..., :MLA_NOPE], jnp.broadcast_to(k_rope, (b, t, MLA_HEADS, MLA_ROPE))], axis=-1)
    return k, kv[..., MLA_NOPE:]


def softmax_attend(q, k, v):
    s = jnp.einsum('bqhd,bkhd->bhqk', q, k).astype(jnp.float32) * MLA_SCALE
    p = jax.nn.softmax(s, axis=-1).astype(v.dtype)
    return jnp.einsum('bhqk,bkhd->bqhd', p, v)


def blocked_attend(q, k, v):
    b, t, h, d = q.shape
    nb = t // Q_BLOCK
    qb = jnp.moveaxis(q.reshape(b, nb, Q_BLOCK, h, d), 1, 0)
    out = lax.map(lambda qi: softmax_attend(qi, k, v), qb)
    return jnp.moveaxis(out, 0, 1).reshape(b, t, h, v.shape[-1])


def mlstm_scan(q, k, v, log_i, log_f, state, emit):
    b, h, t, _ = q.shape
    nc = t // CHUNK

    def chunks(a):
        return jnp.moveaxis(a.reshape(b, h, nc, CHUNK, *a.shape[3:]), 2, 0)

    tril = jnp.tril(jnp.ones((CHUNK, CHUNK), dtype=bool))

    def step(carry, inp):
        c_mat, n_vec, m = carry
        qc, kc, vc, ic, fc = inp
        fcum = jnp.cumsum(fc, axis=-1)
        last = fcum[..., -1]
        src = last[..., None] - fcum + ic
        m_new = jnp.maximum(last + m, jnp.max(src, axis=-1))
        w_src = jnp.exp(src - m_new[..., None])
        w_old = jnp.exp(last + m - m_new)
        c_new = w_old[..., None, None] * c_mat + jnp.einsum('bhs,bhsv,bhsd->bhvd', w_src, vc, kc)
        n_new = w_old[..., None] * n_vec + jnp.einsum('bhs,bhsd->bhd', w_src, kc)
        out = None
        if emit:
            dmat = jnp.where(tril, fcum[..., :, None] - fcum[..., None, :] + ic[..., None, :], -jnp.inf)
            g = fcum + m[..., None]
            m_row = jnp.maximum(g, jnp.max(dmat, axis=-1))
            s = jnp.einsum('bhtd,bhsd->bhts', qc, kc) * jnp.exp(dmat - m_row[..., None])
            w_prev = jnp.exp(g - m_row)
            num = jnp.einsum('bhts,bhsv->bhtv', s, vc) + w_prev[..., None] * jnp.einsum('bhvd,bhtd->bhtv', c_mat, qc)
            den = jnp.sum(s, axis=-1) + w_prev * jnp.einsum('bhd,bhtd->bht', n_vec, qc)
            out = num / jnp.maximum(jnp.abs(den), jnp.exp(-m_row))[..., None]
        return (c_new, n_new, m_new), out

    state, hs = lax.scan(step, state, tuple(chunks(a) for a in (q, k, v, log_i, log_f)))
    if not emit:
        return None, state
    return jnp.moveaxis(hs, 0, 2).reshape(b, h, t, -1), state


def mlstm_dir(p, d, state, reverse, emit):
    arrs = (p['ml_q'], p['ml_k'], p['ml_v'], p['ml_i'][:, d], p['ml_f'][:, d])
    if reverse:
        arrs = tuple(jnp.flip(a, axis=2) for a in arrs)
    h, state = mlstm_scan(*arrs, state, emit)
    if reverse and emit:
        h = jnp.flip(h, axis=2)
    return h, state


def ssd_scan(x, dt, bm, cm, a, state, emit):
    b, t = x.shape[:2]
    nc = t // CHUNK

    def chunks(arr):
        return jnp.moveaxis(arr.reshape(b, nc, CHUNK, *arr.shape[2:]), 1, 0)

    tril = jnp.tril(jnp.ones((CHUNK, CHUNK), dtype=bool))[None, :, :, None, None]

    def step(hst, inp):
        xc, dtc, bc, cc = inp
        acum = jnp.cumsum(dtc * a, axis=1)
        last = acum[:, -1]
        w_src = jnp.exp(last[:, None] - acum) * dtc
        h_new = jnp.exp(last)[..., None, None] * hst + jnp.einsum('bsgn,bsgh,bsghp->bghnp', bc, w_src, xc)
        out = None
        if emit:
            seg = jnp.exp(jnp.where(tril, acum[:, :, None] - acum[:, None, :], -jnp.inf))
            cb = jnp.einsum('btgn,bsgn->btsg', cc, bc)
            y = jnp.einsum('btsgh,bsghp->btghp', seg * cb[..., None] * dtc[:, None], xc)
            out = y + jnp.einsum('btgn,bghnp->btghp', cc, hst) * jnp.exp(acum)[..., None]
        return h_new, out

    state, ys = lax.scan(step, state, tuple(chunks(arr) for arr in (x, dt, bm, cm)))
    if not emit:
        return None, state
    return jnp.moveaxis(ys, 0, 1).reshape(x.shape), state


def ssd_dir(p, a, d, state, reverse, emit):
    arrs = (p['ssd_x'], p['ssd_dt'][:, :, d], p['ssd_b'], p['ssd_c'])
    if reverse:
        arrs = tuple(jnp.flip(arr, axis=1) for arr in arrs)
    y, state = ssd_scan(*arrs, a, state, emit)
    if reverse and emit:
        y = jnp.flip(y, axis=1)
    return y, state


def stream_prep(hn, lp, rope, queries):
    b, t, _ = hn.shape
    f32 = jnp.float32
    (u_q, u_kv, u_kr, u_qk, u_v, u_o, u_if, u_z, u_xbc, u_dt, u_g) = split_cols(hn @ lp['w_in'])

    def heads(a, n):
        return jnp.moveaxis(a.reshape(b, t, n, -1), 2, 1).astype(f32)

    mla_k, mla_v = mla_keys_values(u_kv, u_kr, lp['mla_kvnorm_g'], lp['mla_w_ukv'], rope)
    qk = jax.nn.silu(dwconv_centred(u_qk, lp['ml_conv_w'], lp['ml_conv_b']))
    ml_q, ml_k = jnp.split(qk, 2, axis=-1)
    gates = u_if.reshape(b, t, 2, 2, ML_HEADS).astype(f32) + lp['ml_gate_b']
    gates = jnp.moveaxis(gates, 1, -1)
    xbc = jax.nn.silu(dwconv_centred(u_xbc, lp['ssd_conv_w'], lp['ssd_conv_b']))
    s_x, s_b, s_c = jnp.split(xbc, [SSD_INNER, SSD_INNER + SSD_BC_W], axis=-1)
    dt = jax.nn.softplus(u_dt.reshape(b, t, 2, SSD_GROUPS, SSD_HPG).astype(f32) + lp['ssd_dt_bias'])
    return dict(
        mla_q=mla_queries(u_q, lp['mla_qnorm_g'], lp['mla_w_uq'], rope) if queries else None,
        mla_k=mla_k, mla_v=mla_v,
        ml_q=heads(ml_q, ML_HEADS), ml_k=heads(ml_k, ML_HEADS) * (ML_QK ** -0.5), ml_v=heads(u_v, ML_HEADS),
        ml_i=gates[:, :, 0], ml_f=jax.nn.log_sigmoid(gates[:, :, 1]), ml_o=u_o,
        ssd_x=s_x.reshape(b, t, SSD_GROUPS, SSD_HPG, SSD_P).astype(f32),
        ssd_b=s_b.reshape(b, t, SSD_GROUPS, SSD_N).astype(f32),
        ssd_c=s_c.reshape(b, t, SSD_GROUPS, SSD_N).astype(f32),
        ssd_dt=dt, ssd_z=u_z, gates=u_g)


def stream_out(att, h_ml, y_ssd, p, lp):
    b, t = p['gates'].shape[:2]
    dtype = p['gates'].dtype
    a_out = att.reshape(b, t, MLA_OUT_W)
    m_out = rmsnorm(jnp.moveaxis(h_ml, 1, 2), lp['ml_norm_g']).reshape(b, t, ML_V_W).astype(dtype) * jax.nn.sigmoid(p['ml_o'])
    y = (y_ssd + lp['ssd_d'][..., None] * p['ssd_x']).reshape(b, t, SSD_GROUPS, SSD_HPG * SSD_P)
    y = y * jax.nn.silu(p['ssd_z'].astype(jnp.float32).reshape(b, t, SSD_GROUPS, SSD_HPG * SSD_P))
    s_out = rmsnorm(y, lp['ssd_norm_g']).reshape(b, t, SSD_INNER).astype(dtype)
    g = jax.nn.sigmoid(p['gates']).reshape(b, t, N_BRANCH, D_MODEL)
    merged = (g[:, :, 0] * (a_out @ lp['w_br_mla']) + g[:, :, 1] * (m_out @ lp['w_br_ml'])
              + g[:, :, 2] * (s_out @ lp['w_br_ssd']))
    return merged @ lp['w_out']


def mixer_layer(pc, pl, lp, emit_ctx):
    b = pl['ssd_x'].shape[0]
    f32 = jnp.float32
    k_all = jnp.concatenate([pc['mla_k'], pl['mla_k']], axis=1)
    v_all = jnp.concatenate([pc['mla_v'], pl['mla_v']], axis=1)
    att_l = blocked_attend(pl['mla_q'], k_all, v_all)
    ml_zero = (jnp.zeros((b, ML_HEADS, ML_V, ML_QK), f32), jnp.zeros((b, ML_HEADS, ML_QK), f32),
               jnp.zeros((b, ML_HEADS), f32))
    ssd_zero = jnp.zeros((b, SSD_GROUPS, SSD_HPG, SSD_N, SSD_P), f32)
    ml_l, ml_c, ssd_l, ssd_c = [], [], [], []
    for d in range(2):
        rev = d == 1
        hc, st = mlstm_dir(pc, d, ml_zero, rev, emit_ctx)
        hl, _ = mlstm_dir(pl, d, st, rev, True)
        yc, sst = ssd_dir(pc, lp['ssd_a'][d], d, ssd_zero, rev, emit_ctx)
        yl, _ = ssd_dir(pl, lp['ssd_a'][d], d, sst, rev, True)
        ml_l.append(hl)
        ml_c.append(hc)
        ssd_l.append(yl)
        ssd_c.append(yc)
    out_l = stream_out(att_l, ml_l[0] + ml_l[1], ssd_l[0] + ssd_l[1], pl, lp)
    if not emit_ctx:
        return out_l, None
    att_c = softmax_attend(pc['mla_q'], pc['mla_k'], pc['mla_v'])
    out_c = stream_out(att_c, ml_c[0] + ml_c[1], ssd_c[0] + ssd_c[1], pc, lp)
    return out_l, out_c


def clamped_swiglu(gu):
    glu, lin = jnp.split(gu, 2, axis=-1)
    glu = jnp.minimum(glu, SWIGLU_LIMIT)
    lin = jnp.clip(lin, -SWIGLU_LIMIT, SWIGLU_LIMIT)
    return glu * jax.nn.sigmoid(SWIGLU_ALPHA * glu) * (lin + 1)


def moe_ffn(h, w_router, b_router, w_up, b_up, w_down, b_down):
    t, d = h.shape
    logits = (h @ w_router).astype(jnp.float32) + b_router
    top_v, top_i = lax.top_k(logits, TOP_K)
    gate = jax.nn.softmax(top_v, axis=-1)
    flat_e = top_i.reshape(-1)
    flat_tok = jnp.repeat(jnp.arange(t), TOP_K)
    order = jnp.argsort(flat_e)
    se = flat_e[order]
    counts = jnp.bincount(flat_e, length=N_EXPERTS)
    start = jnp.cumsum(counts) - counts
    padded = (counts + MOE_BLOCK - 1) // MOE_BLOCK * MOE_BLOCK
    pad_end = jnp.cumsum(padded)
    pad_start = pad_end - padded
    dest = pad_start[se] + jnp.arange(t * TOP_K) - start[se]
    n_rows = -(-(t * TOP_K + N_EXPERTS * (MOE_BLOCK - 1)) // MOE_BLOCK) * MOE_BLOCK
    n_blocks = n_rows // MOE_BLOCK
    row_tok = jnp.full((n_rows,), t, dtype=flat_tok.dtype).at[dest].set(flat_tok[order])
    row_w = jnp.zeros((n_rows,), gate.dtype).at[dest].set(gate.reshape(-1)[order])
    block_e = jnp.minimum(jnp.searchsorted(pad_end, jnp.arange(n_blocks) * MOE_BLOCK, side='right'), N_EXPERTS - 1)
    h_pad = jnp.concatenate([h, jnp.zeros((1, d), h.dtype)], axis=0)

    def run_block(args):
        tok, e = args
        gu = h_pad[tok] @ w_up[e] + b_up[e]
        return clamped_swiglu(gu) @ w_down[e] + b_down[e]

    out = lax.map(run_block, (row_tok.reshape(n_blocks, MOE_BLOCK), block_e)).reshape(n_rows, d)
    out = out * row_w[:, None].astype(out.dtype)
    return jax.ops.segment_sum(out, row_tok, num_segments=t + 1)[:t]


def setup_inputs(seed: int = 0) -> dict:
    key = jax.random.key(seed)
    ks = iter(jax.random.split(key, 40))
    L, D = DEPTH, D_MODEL

    def nrm(shape, scale):
        return jax.random.normal(next(ks), shape, jnp.float32) * scale

    x = nrm((BATCH, SEQ, D), 1.0)
    c = nrm((BATCH, D), 1.0)
    ctx = nrm((BATCH, CTX_LEN, D), 1.0)
    c_ctx = nrm((D,), 1.0)
    w_ada = nrm((L, D, 6 * D), 0.5 * D ** -0.5)
    b_ada = nrm((L, 6 * D), 0.02)
    norm1_g = 1.0 + nrm((L, D), 0.02)
    w_in = nrm((L, D, D_IN), D ** -0.5)
    mla_qnorm_g = 1.0 + nrm((L, MLA_Q_LORA), 0.02)
    mla_w_uq = nrm((L, MLA_Q_LORA, MLA_HEADS * (MLA_NOPE + MLA_ROPE)), MLA_Q_LORA ** -0.5)
    mla_kvnorm_g = 1.0 + nrm((L, MLA_KV_LORA), 0.02)
    mla_w_ukv = nrm((L, MLA_KV_LORA, MLA_HEADS * (MLA_NOPE + MLA_V)), MLA_KV_LORA ** -0.5)
    ml_conv_w = nrm((L, CONV_W, 2 * ML_QK_W), CONV_W ** -0.5)
    ml_conv_b = nrm((L, 2 * ML_QK_W), 0.02)
    i_bias = nrm((L, 2, ML_HEADS), 0.1)
    f_bias = jnp.linspace(3.0, 6.0, ML_HEADS, dtype=jnp.float32) + nrm((L, 2, ML_HEADS), 0.1)
    ml_gate_b = jnp.stack([i_bias, f_bias], axis=2)
    ml_norm_g = 1.0 + nrm((L, ML_HEADS, ML_V), 0.02)
    ssd_conv_w = nrm((L, CONV_W, SSD_INNER + 2 * SSD_BC_W), CONV_W ** -0.5)
    ssd_conv_b = nrm((L, SSD_INNER + 2 * SSD_BC_W), 0.02)
    dt0 = jnp.exp(jax.random.uniform(next(ks), (L, 2, SSD_HEADS), jnp.float32,
                                     minval=math.log(1e-3), maxval=math.log(1e-1)))
    ssd_dt_bias = dt0 + jnp.log(-jnp.expm1(-dt0))
    ssd_a_log = jnp.log(jax.random.uniform(next(ks), (L, 2, SSD_HEADS), jnp.float32, minval=1.0, maxval=16.0))
    ssd_d = 1.0 + nrm((L, SSD_HEADS), 0.1)
    ssd_norm_g = 1.0 + nrm((L, SSD_GROUPS, SSD_HPG * SSD_P), 0.02)
    w_br_mla = nrm((L, MLA_OUT_W, D), MLA_OUT_W ** -0.5)
    w_br_ml = nrm((L, ML_V_W, D), ML_V_W ** -0.5)
    w_br_ssd = nrm((L, SSD_INNER, D), SSD_INNER ** -0.5)
    w_out = nrm((L, D, D), D ** -0.5)
    norm2_g = 1.0 + nrm((L, D), 0.02)
    w_router = nrm((L, D, N_EXPERTS), D ** -0.5)
    b_router = nrm((L, N_EXPERTS), 0.01)
    w_up = nrm((L, N_EXPERTS, D, 2 * D_FF), D ** -0.5)
    b_up = nrm((L, N_EXPERTS, 2 * D_FF), 0.02)
    w_down = nrm((L, N_EXPERTS, D_FF, D), D_FF ** -0.5)
    b_down = nrm((L, N_EXPERTS, D), 0.02)
    final_g = 1.0 + nrm((D,), 0.02)
    return {'x': x, 'c': c, 'ctx': ctx, 'c_ctx': c_ctx, 'w_ada': w_ada, 'b_ada': b_ada, 'norm1_g': norm1_g,
            'w_in': w_in, 'mla_qnorm_g': mla_qnorm_g, 'mla_w_uq': mla_w_uq, 'mla_kvnorm_g': mla_kvnorm_g,
            'mla_w_ukv': mla_w_ukv, 'ml_conv_w': ml_conv_w, 'ml_conv_b': ml_conv_b, 'ml_gate_b': ml_gate_b,
            'ml_norm_g': ml_norm_g, 'ssd_conv_w': ssd_conv_w, 'ssd_conv_b': ssd_conv_b, 'ssd_dt_bias': ssd_dt_bias,
            'ssd_a_log': ssd_a_log, 'ssd_d': ssd_d, 'ssd_norm_g': ssd_norm_g, 'w_br_mla': w_br_mla,
            'w_br_ml': w_br_ml, 'w_br_ssd': w_br_ssd, 'w_out': w_out, 'norm2_g': norm2_g, 'w_router': w_router,
            'b_router': b_router, 'w_up': w_up, 'b_up': b_up, 'w_down': w_down, 'b_down': b_down,
            'final_g': final_g}


def reference(x, c, ctx, c_ctx, w_ada, b_ada, norm1_g, w_in, mla_qnorm_g, mla_w_uq, mla_kvnorm_g, mla_w_ukv,
              ml_conv_w, ml_conv_b, ml_gate_b, ml_norm_g, ssd_conv_w, ssd_conv_b, ssd_dt_bias, ssd_a_log, ssd_d,
              ssd_norm_g, w_br_mla, w_br_ml, w_br_ssd, w_out, norm2_g, w_router, b_router, w_up, b_up, w_down,
              b_down, final_g):
    f32 = jnp.float32
    n_lat, d_model = x.shape[1], x.shape[2]
    rope = axial_rope(n_lat)
    xl, xc = x, ctx
    for l in range(DEPTH):
        last = l == DEPTH - 1
        lp = dict(w_in=w_in[l], mla_qnorm_g=mla_qnorm_g[l], mla_w_uq=mla_w_uq[l], mla_kvnorm_g=mla_kvnorm_g[l],
                  mla_w_ukv=mla_w_ukv[l], ml_conv_w=ml_conv_w[l], ml_conv_b=ml_conv_b[l], ml_gate_b=ml_gate_b[l],
                  ml_norm_g=ml_norm_g[l], ssd_conv_w=ssd_conv_w[l], ssd_conv_b=ssd_conv_b[l],
                  ssd_dt_bias=ssd_dt_bias[l].astype(f32).reshape(2, SSD_GROUPS, SSD_HPG),
                  ssd_a=-jnp.exp(ssd_a_log[l].astype(f32)).reshape(2, SSD_GROUPS, SSD_HPG),
                  ssd_d=ssd_d[l].reshape(SSD_GROUPS, SSD_HPG), ssd_norm_g=ssd_norm_g[l],
                  w_br_mla=w_br_mla[l], w_br_ml=w_br_ml[l], w_br_ssd=w_br_ssd[l], w_out=w_out[l])
        mod_l = (jax.nn.silu(c) @ w_ada[l] + b_ada[l])[:, None, :]
        mod_c = (jax.nn.silu(c_ctx) @ w_ada[l] + b_ada[l])[None, None, :]
        sh1_l, sc1_l, g1_l, sh2_l, sc2_l, g2_l = jnp.split(mod_l, 6, axis=-1)
        sh1_c, sc1_c, g1_c, sh2_c, sc2_c, g2_c = jnp.split(mod_c, 6, axis=-1)
        pl = stream_prep(modulate(xl, norm1_g[l], sh1_l, sc1_l), lp, rope, True)
        pc = stream_prep(modulate(xc, norm1_g[l], sh1_c, sc1_c), lp, None, not last)
        out_l, out_c = mixer_layer(pc, pl, lp, not last)
        xl = xl + g1_l * out_l
        moe_args = (w_router[l], b_router[l], w_up[l], b_up[l], w_down[l], b_down[l])
        hl2 = modulate(xl, norm2_g[l], sh2_l, sc2_l).reshape(-1, d_model)
        if last:
            xl = xl + g2_l * moe_ffn(hl2, *moe_args).reshape(xl.shape)
        else:
            xc = xc + g1_c * out_c
            hc2 = modulate(xc, norm2_g[l], sh2_c, sc2_c).reshape(-1, d_model)
            n_c = hc2.shape[0]
            f = moe_ffn(jnp.concatenate([hc2, hl2], axis=0), *moe_args)
            xc = xc + g2_c * f[:n_c].reshape(xc.shape)
            xl = xl + g2_l * f[n_c:].reshape(xl.shape)
    return rmsnorm(xl, final_g)
```

```python
import functools
import math

import jax
import jax.numpy as jnp
import numpy as np
from jax import lax
from jax.experimental import pallas as pl
from jax.experimental.pallas import tpu as pltpu

F32 = jnp.float32
BF16 = jnp.bfloat16
HIGHEST = lax.Precision.HIGHEST

D_MODEL = 1024
GRID_W = 64
EPS = 1e-6
ROPE_BASE = 10000.0
CONV_W = 5
MLA_HEADS = 8
MLA_NOPE = 64
MLA_ROPE = 32
MLA_V = 64
MLA_Q_LORA = 384
MLA_KV_LORA = 256
MLA_SCALE = (MLA_NOPE + MLA_ROPE) ** -0.5
ML_HEADS = 4
ML_QK = 64
ML_V = 128
ML_QK_W = ML_HEADS * ML_QK
ML_V_W = ML_HEADS * ML_V
SSD_HEADS = 16
SSD_P = 64
SSD_N = 128
SSD_GROUPS = 4
SSD_HPG = SSD_HEADS // SSD_GROUPS
SSD_INNER = SSD_HEADS * SSD_P
SSD_BC_W = SSD_GROUPS * SSD_N
N_EXPERTS = 32
TOP_K = 4
D_FF = 1024
SWIGLU_LIMIT = 7.0
SWIGLU_ALPHA = 1.702

LANES = 128
HEAD_PAD = 128
VMEM_LIMIT = 56 * 1024 * 1024

ROW_TILE = 256
CHUNK = 128
MOE_ROWS = 256

OFF_Q = 0
OFF_KV = OFF_Q + MLA_Q_LORA
OFF_SM = OFF_KV + MLA_KV_LORA
OFF_CV = OFF_SM + LANES
CV_W = 2 * ML_QK_W + SSD_INNER + 2 * SSD_BC_W
OFF_V = OFF_CV + CV_W
OFF_O = OFF_V + ML_V_W
OFF_Z = OFF_O + ML_V_W
OFF_G = OFF_Z + SSD_INNER
IN_PACKED = OFF_G + 3 * D_MODEL
SM_KR = 0
SM_KROT = MLA_ROPE
SM_IF = 2 * MLA_ROPE
SM_DT = SM_IF + 4 * ML_HEADS
CV_X = 0
CV_B = CV_X + SSD_INNER
CV_C = CV_B + SSD_BC_W
CV_Q = CV_C + SSD_BC_W
CV_K = CV_Q + ML_QK_W
CV_SSD_W = CV_Q


def _params(sem, vmem=VMEM_LIMIT):
    return pltpu.CompilerParams(dimension_semantics=sem, vmem_limit_bytes=vmem)


def _dot(a, b):
    return jnp.dot(a, b, preferred_element_type=F32)


def _dot_nt(a, b):
    return lax.dot_general(a, b, (((1,), (1,)), ((), ())), preferred_element_type=F32)


def _dot_hi(a, b):
    return jnp.dot(a, b, preferred_element_type=F32, precision=HIGHEST)


def _sigmoid(x):
    return 1.0 / (1.0 + jnp.exp(-x))


def _softplus(x):
    return jnp.maximum(x, 0.0) + jnp.log(1.0 + jnp.exp(-jnp.abs(x)))


def _log_sigmoid(x):
    return jnp.minimum(x, 0.0) - jnp.log(1.0 + jnp.exp(-jnp.abs(x)))


def _rms(x, g):
    return x * lax.rsqrt(jnp.mean(x * x, axis=-1, keepdims=True) + EPS) * g


def _ada_kernel(c_ref, w_ref, b_ref, o_ref):
    c = c_ref[...]
    o_ref[0] = _dot_hi(c * _sigmoid(c), w_ref[0]) + b_ref[0]


def _ada_call(cond, w_ada, b_ada):
    depth, d, n = w_ada.shape
    tn = 1536
    return pl.pallas_call(
        _ada_kernel,
        grid=(depth, n // tn),
        in_specs=[pl.BlockSpec((cond.shape[0], d), lambda l, j: (0, 0)),
                  pl.BlockSpec((1, d, tn), lambda l, j: (l, 0, j)),
                  pl.BlockSpec((1, 1, tn), lambda l, j: (l, 0, j))],
        out_specs=pl.BlockSpec((1, cond.shape[0], tn), lambda l, j: (l, 0, j)),
        out_shape=jax.ShapeDtypeStruct((depth, cond.shape[0], n), F32),
        compiler_params=_params(("arbitrary", "arbitrary")),
        name="ada_mod",
    )(cond, w_ada, b_ada.reshape(depth, 1, n))


def _in_kernel(x_ref, mod_ref, g1_ref, w_ref, wst_ref, gq_ref, wqm_ref, wqr_ref, gkv_ref, wk_ref, wv_ref,
               e_ref, t1_ref, cs_ref, sn_ref,
               q_ref, k_ref, vv_ref, smc_ref, smt_ref, cv_ref, v_ref, o_ref, z_ref, g_ref):
    x = x_ref[...]
    mod = mod_ref[0, 0]
    hn = _rms(x, g1_ref[...]) * (1.0 + mod[1:2]) + mod[0:1]
    hb = hn.astype(BF16)

    def proj(lo, hi):
        return _dot(hb, w_ref[:, lo:hi])

    sm = proj(OFF_SM, OFF_CV)
    smc_ref[...] = sm
    smt_ref[...] = _dot_nt(wst_ref[...], hb)
    cv_ref[...] = proj(OFF_CV, OFF_V).astype(BF16)
    v_ref[...] = proj(OFF_V, OFF_O).astype(BF16)
    o_ref[...] = proj(OFF_O, OFF_Z).astype(BF16)
    z_ref[...] = proj(OFF_Z, OFF_G).astype(BF16)
    g_ref[...] = proj(OFF_G, IN_PACKED).astype(BF16)

    qn = _rms(proj(OFF_Q, OFF_KV), gq_ref[...]).astype(BF16)
    qm = _dot(qn, wqm_ref[...])
    qr = _dot(qn, wqr_ref[...])
    cs = cs_ref[...]
    sn = sn_ref[...]
    for h in range(MLA_HEADS):
        sl = slice(h * HEAD_PAD, (h + 1) * HEAD_PAD)
        q_ref[:, sl] = ((qm[:, sl] * cs + qr[:, sl] * sn) * MLA_SCALE).astype(BF16)
    kvn = _rms(proj(OFF_KV, OFF_SM), gkv_ref[...]).astype(BF16)
    kro = (sm * t1_ref[...]).astype(BF16)
    k_ref[...] = (_dot(kvn, wk_ref[...]) + _dot(kro, e_ref[...])).astype(BF16)
    vv_ref[...] = _dot(kvn, wv_ref[...]).astype(BF16)


def _in_call(x, mod, g1, wp, wst, gq, wqm, wqr, gkv, wk, wv, emat, t1, cs, sn, tiles_per_b):
    t, d = x.shape
    n_tiles = t // ROW_TILE

    def const(shape):
        return pl.BlockSpec(shape, lambda i: (0,) * len(shape), pipeline_mode=pl.Buffered(1))

    def rows(w):
        return pl.BlockSpec((ROW_TILE, w), lambda i: (i, 0))

    def tab():
        return pl.BlockSpec((ROW_TILE, LANES), lambda i: (i % tiles_per_b, 0))

    widths = [MLA_HEADS * HEAD_PAD, MLA_HEADS * HEAD_PAD, MLA_HEADS * MLA_V, LANES, None, CV_W, ML_V_W, ML_V_W,
              SSD_INNER, 3 * D_MODEL]
    dtypes = [BF16, BF16, BF16, F32, F32, BF16, BF16, BF16, BF16, BF16]
    out_specs, out_shape = [], []
    for w, dt in zip(widths, dtypes):
        if w is None:
            out_specs.append(pl.BlockSpec((LANES, ROW_TILE), lambda i: (0, i)))
            out_shape.append(jax.ShapeDtypeStruct((LANES, t), dt))
        else:
            out_specs.append(rows(w))
            out_shape.append(jax.ShapeDtypeStruct((t, w), dt))
    return pl.pallas_call(
        _in_kernel,
        grid=(n_tiles,),
        in_specs=[rows(d),
                  pl.BlockSpec((1, 1, 6, d), lambda i: (i // tiles_per_b, jnp.minimum(i % tiles_per_b, 1), 0, 0)),
                  const((1, d)), const(wp.shape), const(wst.shape), const((1, MLA_Q_LORA)), const(wqm.shape),
                  const(wqr.shape), const((1, MLA_KV_LORA)), const(wk.shape), const(wv.shape), const(emat.shape),
                  tab(), tab(), tab()],
        out_specs=out_specs,
        out_shape=out_shape,
        compiler_params=_params(("arbitrary",)),
        name="in_proj",
    )(x, mod, g1, wp, wst, gq, wqm, wqr, gkv, wk, wv, emat, t1, cs, sn)


def _conv_kernel(x_ref, w_ref, b_ref, s_ref, o_ref, *, n_ctx):
    x = x_ref[0].astype(F32)
    s = x.shape[0]
    t = lax.broadcasted_iota(jnp.int32, x.shape, 0)
    is_ctx = t < n_ctx
    pos = jnp.where(is_ctx, t, t - n_ctx)
    length = jnp.where(is_ctx, n_ctx, s - n_ctx)
    acc = jnp.zeros_like(x) + b_ref[...]
    for j in range(CONV_W):
        d = j - CONV_W // 2
        xs = x if d == 0 else pltpu.roll(x, (-d) % s, 0)
        valid = (pos + d >= 0) & (pos + d < length)
        acc = acc + jnp.where(valid, xs, 0.0) * w_ref[j:j + 1, :]
    o_ref[0] = (acc * _sigmoid(acc) * s_ref[...]).astype(o_ref.dtype)


def _conv_call(cv, w, b, post, n_ctx):
    bsz, s, c = cv.shape
    return pl.pallas_call(
        functools.partial(_conv_kernel, n_ctx=n_ctx),
        grid=(bsz, c // LANES),
        in_specs=[pl.BlockSpec((1, s, LANES), lambda bi, j: (bi, 0, j)),
                  pl.BlockSpec((8, LANES), lambda bi, j: (0, j)),
                  pl.BlockSpec((1, LANES), lambda bi, j: (0, j)),
                  pl.BlockSpec((1, LANES), lambda bi, j: (0, j))],
        out_specs=pl.BlockSpec((1, s, LANES), lambda bi, j: (bi, 0, j)),
        out_shape=jax.ShapeDtypeStruct(cv.shape, BF16),
        compiler_params=_params(("arbitrary", "arbitrary")),
        name="dwconv_silu",
    )(cv, w, b, post)


def _attn_kernel(q_ref, k_ref, v_ref, o_ref, *, n_ctx):
    qi = pl.program_id(2)

    def attend(n_keys):
        outs = []
        for j in range(2):
            sl = slice(j * HEAD_PAD, (j + 1) * HEAD_PAD)
            s = _dot_nt(q_ref[0, :, sl], k_ref[0, 0:n_keys, sl])
            m = jnp.max(s, axis=-1, keepdims=True)
            p = jnp.exp(s - m)
            l = jnp.sum(p, axis=-1, keepdims=True)
            outs.append(_dot(p.astype(BF16), v_ref[0, 0:n_keys, :]) / l)
        lane = lax.broadcasted_iota(jnp.int32, outs[0].shape, 1)
        o_ref[0] = jnp.where(lane < MLA_V, outs[0], outs[1]).astype(o_ref.dtype)

    @pl.when(qi == 0)
    def _():
        attend(n_ctx)

    @pl.when(qi != 0)
    def _():
        attend(k_ref.shape[1])


def _attn_call(q, k, v, n_ctx):
    bsz, s, _ = q.shape
    assert n_ctx == ROW_TILE
    pair = 2 * HEAD_PAD
    return pl.pallas_call(
        functools.partial(_attn_kernel, n_ctx=n_ctx),
        grid=(bsz, MLA_HEADS // 2, s // ROW_TILE),
        in_specs=[pl.BlockSpec((1, ROW_TILE, pair), lambda b, h, i: (b, i, h)),
                  pl.BlockSpec((1, s, pair), lambda b, h, i: (b, 0, h)),
                  pl.BlockSpec((1, s, 2 * MLA_V), lambda b, h, i: (b, 0, h))],
        out_specs=pl.BlockSpec((1, ROW_TILE, 2 * MLA_V), lambda b, h, i: (b, i, h)),
        out_shape=jax.ShapeDtypeStruct((bsz, s, MLA_HEADS * MLA_V), BF16),
        compiler_params=_params(("arbitrary", "arbitrary", "arbitrary")),
        name="mla_attention",
    )(q, k, v)


def _tri(n, upper):
    r = lax.broadcasted_iota(jnp.int32, (n, n), 0)
    c = lax.broadcasted_iota(jnp.int32, (n, n), 1)
    return (r <= c) if upper else (r >= c)


def _bwd_chunk(j, nc_ctx, nc):
    return jnp.where(j < nc_ctx, nc_ctx - 1 - j, nc + nc_ctx - 1 - j)


def _mlstm_kernel(qkf_ref, vf_ref, gcf_ref, grf_ref, qkb_ref, vb_ref, gcb_ref, grb_ref, brow_ref, bcol_ref,
                  hf_ref, hb_ref, st_ref, m_ref):
    j = pl.program_id(1)
    L = CHUNK

    @pl.when(j == 0)
    def _():
        st_ref[...] = jnp.zeros_like(st_ref)
        m_ref[...] = jnp.zeros_like(m_ref)

    lane = lax.broadcasted_iota(jnp.int32, (L, LANES), 1)
    ones_col = jnp.where(lane == 0, 1.0, 0.0).astype(F32)
    lane_qk = lax.broadcasted_iota(jnp.int32, (L, ML_QK_W), 1)

    def direction(d, qk_ref, v_ref, gc_ref, gr_ref, out_ref):
        rev = d == 1
        gc = gc_ref[...] + brow_ref[...]
        gr = gr_ref[...] + bcol_ref[...]
        fc_all = _dot_hi(jnp.where(_tri(L, rev), 1.0, 0.0).astype(F32), _log_sigmoid(gc))
        fr_all = _dot_hi(_log_sigmoid(gr), jnp.where(_tri(L, not rev), 1.0, 0.0).astype(F32))
        mask = _tri(L, rev)
        end = 0 if rev else L - 1
        q_all = qk_ref[0, :, 0:ML_QK_W]
        k_all = qk_ref[0, :, ML_QK_W:2 * ML_QK_W]
        kt_all = jnp.transpose(k_all.astype(F32)).astype(BF16)
        for h in range(ML_HEADS):
            p = d * ML_HEADS + h
            li_lane = SM_IF + d * 2 * ML_HEADS + h
            lf_lane = li_lane + ML_HEADS
            fc = fc_all[:, lf_lane:lf_lane + 1]
            fr = fr_all[lf_lane:lf_lane + 1, :]
            ic = gc[:, li_lane:li_lane + 1]
            ir = gr[li_lane:li_lane + 1, :]
            last = fc[end:end + 1, :]
            m_old = m_ref[p][0:1, 0:1]
            m_new = jnp.maximum(last + m_old, jnp.max(last - fr + ir, axis=-1, keepdims=True))
            w_src = jnp.exp(last - fc + ic - m_new)
            w_old = jnp.exp(last + m_old - m_new)
            dmat = jnp.where(mask, fc - fr + ir, -jnp.inf)
            g = fc + m_old
            m_row = jnp.maximum(g, jnp.max(dmat, axis=-1, keepdims=True))
            q_h = jnp.where((lane_qk >= h * ML_QK) & (lane_qk < (h + 1) * ML_QK), q_all, 0).astype(BF16)
            s = _dot_nt(q_h, k_all) * jnp.exp(dmat - m_row)
            v_ext = jnp.concatenate([v_ref[0, :, h * ML_V:(h + 1) * ML_V].astype(F32), ones_col], axis=1)
            state = st_ref[d]
            tot = (_dot(s.astype(BF16), v_ext.astype(BF16))
                   + jnp.exp(g - m_row) * _dot(q_h, state.astype(BF16)))
            den = tot[:, ML_V:ML_V + 1]
            out_ref[0, :, h * ML_V:(h + 1) * ML_V] = (
                tot[:, :ML_V] / jnp.maximum(jnp.abs(den), jnp.exp(-m_row))).astype(out_ref.dtype)
            rows = slice(h * ML_QK, (h + 1) * ML_QK)
            st_ref[d, rows, :] = w_old * state[rows, :] + _dot(kt_all[rows, :], (w_src * v_ext).astype(BF16))
            m_ref[p] = jnp.broadcast_to(m_new, m_ref.shape[1:])

    direction(0, qkf_ref, vf_ref, gcf_ref, grf_ref, hf_ref)
    direction(1, qkb_ref, vb_ref, gcb_ref, grb_ref, hb_ref)


def _mlstm_call(cvo, v, smc, smt, brow, bcol, n_ctx):
    bsz, s, _ = cvo.shape
    nc, nc_ctx = s // CHUNK, n_ctx // CHUNK
    qk_w = 2 * ML_QK_W

    def fwd(b, j):
        return j

    def bwd(b, j):
        return _bwd_chunk(j, nc_ctx, nc)

    def specs(ch):
        return [pl.BlockSpec((1, CHUNK, qk_w), lambda b, j: (b, ch(b, j), CV_Q // qk_w)),
                pl.BlockSpec((1, CHUNK, ML_V_W), lambda b, j: (b, ch(b, j), 0)),
                pl.BlockSpec((CHUNK, LANES), lambda b, j: (b * nc + ch(b, j), 0)),
                pl.BlockSpec((LANES, CHUNK), lambda b, j: (0, b * nc + ch(b, j)))]

    out = jax.ShapeDtypeStruct((bsz, s, ML_V_W), F32)
    return pl.pallas_call(
        _mlstm_kernel,
        grid=(bsz, nc),
        in_specs=specs(fwd) + specs(bwd) + [pl.BlockSpec((1, LANES), lambda b, j: (0, 0)),
                                            pl.BlockSpec((LANES, 1), lambda b, j: (0, 0))],
        out_specs=[pl.BlockSpec((1, CHUNK, ML_V_W), lambda b, j: (b, fwd(b, j), 0)),
                   pl.BlockSpec((1, CHUNK, ML_V_W), lambda b, j: (b, bwd(b, j), 0))],
        out_shape=[out, out],
        scratch_shapes=[pltpu.VMEM((2, ML_QK_W, ML_V + LANES), F32),
                        pltpu.VMEM((2 * ML_HEADS, 8, LANES), F32)],
        compiler_params=_params(("arbitrary", "arbitrary")),
        name="mlstm_scan",
    )(cvo, v, smc, smt, cvo, v, smc, smt, brow, bcol)


def _ssd_kernel(cvf_ref, gcf_ref, grf_ref, cvb_ref, gcb_ref, grb_ref, brow_ref, bcol_ref, arow_ref, acol_ref,
                yf_ref, yb_ref, st_ref):
    j = pl.program_id(1)
    L = CHUNK
    gw = SSD_HPG * SSD_P

    @pl.when(j == 0)
    def _():
        st_ref[...] = jnp.zeros_like(st_ref)

    lane_g = lax.broadcasted_iota(jnp.int32, (L, gw), 1)

    def per_head(cols):
        out = cols[SSD_HPG - 1]
        for i in range(SSD_HPG - 2, -1, -1):
            out = jnp.where(lane_g[0:cols[0].shape[0]] < (i + 1) * SSD_P, cols[i], out)
        return out

    def direction(d, cv_ref, gc_ref, gr_ref, out_ref):
        rev = d == 1
        dtc = _softplus(gc_ref[...] + brow_ref[...])
        dtr = _softplus(gr_ref[...] + bcol_ref[...])
        ac_all = _dot_hi(jnp.where(_tri(L, rev), 1.0, 0.0).astype(F32), dtc * arow_ref[...])
        ar_all = _dot_hi(dtr * acol_ref[...], jnp.where(_tri(L, not rev), 1.0, 0.0).astype(F32))
        mask = _tri(L, rev)
        end = 0 if rev else L - 1
        for gi in range(SSD_GROUPS):
            x_g = cv_ref[0, :, CV_X + gi * gw:CV_X + (gi + 1) * gw]
            b_g = cv_ref[0, :, CV_B + gi * SSD_N:CV_B + (gi + 1) * SSD_N]
            c_g = cv_ref[0, :, CV_C + gi * SSD_N:CV_C + (gi + 1) * SSD_N]
            cb = _dot_nt(c_g, b_g)
            state = st_ref[d, gi]
            y = jnp.zeros((L, gw), F32)
            e_cols, w_cols, last_cols = [], [], []
            for hg in range(SSD_HPG):
                ln = SM_DT + d * SSD_HEADS + gi * SSD_HPG + hg
                ac = ac_all[:, ln:ln + 1]
                ar = ar_all[ln:ln + 1, :]
                last = ac[end:end + 1, :]
                seg = jnp.exp(jnp.where(mask, ac - ar, -jnp.inf))
                mm = (seg * cb * dtr[ln:ln + 1, :]).astype(BF16)
                y = jnp.where((lane_g >= hg * SSD_P) & (lane_g < (hg + 1) * SSD_P), _dot(mm, x_g), y)
                e_cols.append(jnp.exp(ac))
                w_cols.append(jnp.exp(last - ac) * dtc[:, ln:ln + 1])
                last_cols.append(jnp.exp(last))
            y = y + _dot(c_g, state.astype(BF16)) * per_head(e_cols)
            out_ref[0, :, gi * gw:(gi + 1) * gw] = y.astype(out_ref.dtype)
            bt = jnp.transpose(b_g.astype(F32)).astype(BF16)
            xw = (x_g.astype(F32) * per_head(w_cols)).astype(BF16)
            st_ref[d, gi] = per_head(last_cols) * state + _dot(bt, xw)

    direction(0, cvf_ref, gcf_ref, grf_ref, yf_ref)
    direction(1, cvb_ref, gcb_ref, grb_ref, yb_ref)


def _ssd_call(cvo, smc, smt, brow, bcol, arow, acol, n_ctx):
    bsz, s, _ = cvo.shape
    nc, nc_ctx = s // CHUNK, n_ctx // CHUNK

    def fwd(b, j):
        return j

    def bwd(b, j):
        return _bwd_chunk(j, nc_ctx, nc)

    def specs(ch):
        return [pl.BlockSpec((1, CHUNK, CV_SSD_W), lambda b, j: (b, ch(b, j), 0)),
                pl.BlockSpec((CHUNK, LANES), lambda b, j: (b * nc + ch(b, j), 0)),
                pl.BlockSpec((LANES, CHUNK), lambda b, j: (0, b * nc + ch(b, j)))]

    def const(shape):
        return pl.BlockSpec(shape, lambda b, j: (0, 0))

    out = jax.ShapeDtypeStruct((bsz, s, SSD_INNER), F32)
    return pl.pallas_call(
        _ssd_kernel,
        grid=(bsz, nc),
        in_specs=specs(fwd) + specs(bwd) + [const((1, LANES)), const((LANES, 1)), const((1, LANES)),
                                            const((LANES, 1))],
        out_specs=[pl.BlockSpec((1, CHUNK, SSD_INNER), lambda b, j: (b, fwd(b, j), 0)),
                   pl.BlockSpec((1, CHUNK, SSD_INNER), lambda b, j: (b, bwd(b, j), 0))],
        out_shape=[out, out],
        scratch_shapes=[pltpu.VMEM((2, SSD_GROUPS, SSD_N, SSD_HPG * SSD_P), F32)],
        compiler_params=_params(("arbitrary", "arbitrary")),
        name="ssd_scan",
    )(cvo, smc, smt, cvo, smc, smt, brow, bcol, arow, acol)


def _out_kernel(x_ref, mod_ref, att_ref, hf_ref, hb_ref, yf_ref, yb_ref, sx_ref, z_ref, o_ref, g_ref,
                mlg_ref, sd_ref, sg_ref, wa_ref, wm_ref, ws_ref, wo_ref, g2_ref, wr_ref, br_ref,
                xo_ref, h2_ref, lg_ref):
    mod = mod_ref[0, 0]
    hm = hf_ref[...] + hb_ref[...]
    m_parts = []
    for h in range(ML_HEADS):
        sl = slice(h * ML_V, (h + 1) * ML_V)
        m_parts.append(_rms(hm[:, sl], mlg_ref[:, sl]))
    m_out = jnp.concatenate(m_parts, axis=1) * _sigmoid(o_ref[...].astype(F32))
    z = z_ref[...].astype(F32)
    y = (yf_ref[...] + yb_ref[...] + sd_ref[...] * sx_ref[...].astype(F32)) * (z * _sigmoid(z))
    gw = SSD_HPG * SSD_P
    s_parts = []
    for gi in range(SSD_GROUPS):
        sl = slice(gi * gw, (gi + 1) * gw)
        s_parts.append(_rms(y[:, sl], sg_ref[:, sl]))
    s_out = jnp.concatenate(s_parts, axis=1)
    gt = g_ref[...].astype(F32)
    d = D_MODEL
    merged = (_sigmoid(gt[:, 0:d]) * _dot(att_ref[...], wa_ref[...])
              + _sigmoid(gt[:, d:2 * d]) * _dot(m_out.astype(BF16), wm_ref[...])
              + _sigmoid(gt[:, 2 * d:3 * d]) * _dot(s_out.astype(BF16), ws_ref[...]))
    xn = x_ref[...] + mod[2:3] * _dot(merged.astype(BF16), wo_ref[...])
    xo_ref[...] = xn
    h2 = _rms(xn, g2_ref[...]) * (1.0 + mod[4:5]) + mod[3:4]
    h2_ref[...] = h2
    lg_ref[...] = _dot_hi(h2, wr_ref[...]) + br_ref[...]


def _out_call(x, mod, att, hf, hb, yf, yb, cvo, z, o, g, mlg, sd, sg, wa, wm, ws, wo, g2, wr, br, tiles_per_b):
    t, d = x.shape
    n_tiles = t // ROW_TILE

    def const(shape):
        return pl.BlockSpec(shape, lambda i: (0,) * len(shape), pipeline_mode=pl.Buffered(1))

    def rows(w, blk=0):
        return pl.BlockSpec((ROW_TILE, w), lambda i: (i, blk))

    return pl.pallas_call(
        _out_kernel,
        grid=(n_tiles,),
        in_specs=[rows(d),
                  pl.BlockSpec((1, 1, 6, d), lambda i: (i // tiles_per_b, jnp.minimum(i % tiles_per_b, 1), 0, 0)),
                  rows(MLA_HEADS * MLA_V), rows(ML_V_W), rows(ML_V_W), rows(SSD_INNER), rows(SSD_INNER),
                  rows(SSD_INNER, CV_X // SSD_INNER), rows(SSD_INNER), rows(ML_V_W), rows(3 * d),
                  const((1, ML_V_W)), const((1, SSD_INNER)), const((1, SSD_INNER)),
                  const(wa.shape), const(wm.shape), const(ws.shape), const(wo.shape), const((1, d)),
                  const(wr.shape), const((1, LANES))],
        out_specs=[rows(d), rows(d), rows(LANES)],
        out_shape=[jax.ShapeDtypeStruct((t, d), F32), jax.ShapeDtypeStruct((t, d), F32),
                   jax.ShapeDtypeStruct((t, LANES), F32)],
        compiler_params=_params(("arbitrary",)),
        name="mix_out",
    )(x, mod, att, hf, hb, yf, yb, cvo, z, o, g, mlg, sd, sg, wa, wm, ws, wo, g2, wr, br)


def _route_kernel(lg_ref, e_ref, gate_ref, rank_ref, cnt_ref, base_ref):
    i = pl.program_id(0)

    @pl.when(i == 0)
    def _():
        base_ref[...] = jnp.zeros_like(base_ref)

    v = lg_ref[...]
    n = v.shape[0]
    lane = lax.broadcasted_iota(jnp.int32, v.shape, 1)
    v = jnp.where(lane < N_EXPERTS, v, -jnp.inf)
    tops, hots = [], []
    e_out = jnp.zeros(v.shape, jnp.int32)
    for k in range(TOP_K):
        mk = jnp.max(v, axis=-1, keepdims=True)
        idx = jnp.min(jnp.where(v == mk, lane, LANES), axis=-1, keepdims=True)
        hot = lane == idx
        v = jnp.where(hot, -jnp.inf, v)
        tops.append(mk)
        hots.append(hot)
        e_out = jnp.where(lane == k, idx, e_out)
    ex = [jnp.exp(tk - tops[0]) for tk in tops]
    den = ex[0]
    for k in range(1, TOP_K):
        den = den + ex[k]
    gate = jnp.zeros(v.shape, F32)
    for k in range(TOP_K):
        gate = jnp.where(lane == k, ex[k] / den, gate)
    hot_sum = jnp.zeros(v.shape, F32)
    for k in range(TOP_K):
        hot_sum = hot_sum + jnp.where(hots[k], 1.0, 0.0)
    r = lax.broadcasted_iota(jnp.int32, (n, n), 0)
    c = lax.broadcasted_iota(jnp.int32, (n, n), 1)
    before = jnp.where(c < r, 1.0, 0.0).astype(BF16)
    base = base_ref[0:1, :]
    cum = _dot(before, hot_sum.astype(BF16)) + base
    rank = jnp.zeros(v.shape, jnp.int32)
    for k in range(TOP_K):
        rk = jnp.sum(jnp.where(hots[k], cum, 0.0), axis=-1, keepdims=True)
        rank = jnp.where(lane == k, rk.astype(jnp.int32), rank)
    e_ref[...] = e_out
    gate_ref[...] = gate
    rank_ref[...] = rank
    new_base = base + jnp.sum(hot_sum, axis=0, keepdims=True)
    base_ref[...] = jnp.broadcast_to(new_base, base_ref.shape)
    cnt_ref[...] = jnp.broadcast_to(new_base, cnt_ref.shape)


def _route_call(logits):
    t = logits.shape[0]

    def rows():
        return pl.BlockSpec((ROW_TILE, LANES), lambda i: (i, 0))

    return pl.pallas_call(
        _route_kernel,
        grid=(t // ROW_TILE,),
        in_specs=[rows()],
        out_specs=[rows(), rows(), rows(), pl.BlockSpec((8, LANES), lambda i: (0, 0))],
        out_shape=[jax.ShapeDtypeStruct((t, LANES), jnp.int32), jax.ShapeDtypeStruct((t, LANES), F32),
                   jax.ShapeDtypeStruct((t, LANES), jnp.int32), jax.ShapeDtypeStruct((8, LANES), F32)],
        scratch_shapes=[pltpu.VMEM((8, LANES), F32)],
        compiler_params=_params(("arbitrary",)),
        name="moe_route",
    )(logits)


def _moe_kernel(be_ref, nb_ref, tok_ref, tokn_ref, h_hbm, wu_ref, bu_ref, wd_ref, bd_ref, y_ref,
                xbuf, wub, wdb, sem):
    i = pl.program_id(0)
    n_used = nb_ref[0]
    slot = i % 2

    def gather(tref, dst_slot):
        def body(r, carry):
            pltpu.make_async_copy(h_hbm.at[pl.ds(tref[0, 0, r], 1), :], xbuf.at[dst_slot, pl.ds(r, 1), :],
                                  sem.at[dst_slot]).start()
            return carry
        lax.fori_loop(0, MOE_ROWS, body, 0)

    @pl.when(i == 0)
    def _():
        gather(tok_ref, 0)

    @pl.when(i + 1 < n_used)
    def _():
        gather(tokn_ref, 1 - slot)

    @pl.when(i < n_used)
    def _():
        prev = be_ref[jnp.maximum(i - 1, 0)]

        @pl.when((i == 0) | (be_ref[i] != prev))
        def _():
            wub[...] = wu_ref[...].astype(BF16)
            wdb[...] = wd_ref[...].astype(BF16)

        pltpu.make_async_copy(h_hbm.at[pl.ds(0, MOE_ROWS), :], xbuf.at[slot], sem.at[slot]).wait()
        x = xbuf[slot].astype(BF16)
        gu = _dot(x, wub[...]) + bu_ref[...]
        glu = jnp.minimum(gu[:, :D_FF], SWIGLU_LIMIT)
        lin = jnp.clip(gu[:, D_FF:], -SWIGLU_LIMIT, SWIGLU_LIMIT)
        act = glu * _sigmoid(SWIGLU_ALPHA * glu) * (lin + 1.0)
        y_ref[...] = _dot(act.astype(BF16), wdb[...]) + bd_ref[...]

    @pl.when(i >= n_used)
    def _():
        y_ref[...] = jnp.zeros_like(y_ref)


def _moe_call(block_e, n_used, row_tok, h2, w_up, b_up, w_down, b_down, layer):
    n_blocks = block_e.shape[0]
    t, d = h2.shape
    tok3 = row_tok.reshape(n_blocks, 1, MOE_ROWS)
    b_up = b_up.reshape(b_up.shape[0], N_EXPERTS, 1, 2 * D_FF)
    b_down = b_down.reshape(b_down.shape[0], N_EXPERTS, 1, d)
    grid_spec = pltpu.PrefetchScalarGridSpec(
        num_scalar_prefetch=2,
        grid=(n_blocks,),
        in_specs=[pl.BlockSpec((1, 1, MOE_ROWS), lambda i, be, nb: (i, 0, 0), memory_space=pltpu.SMEM),
                  pl.BlockSpec((1, 1, MOE_ROWS), lambda i, be, nb: (jnp.minimum(i + 1, n_blocks - 1), 0, 0),
                               memory_space=pltpu.SMEM),
                  pl.BlockSpec(memory_space=pl.ANY),
                  pl.BlockSpec((None, None, d, 2 * D_FF), lambda i, be, nb: (layer, be[i], 0, 0)),
                  pl.BlockSpec((None, None, 1, 2 * D_FF), lambda i, be, nb: (layer, be[i], 0, 0)),
                  pl.BlockSpec((None, None, D_FF, d), lambda i, be, nb: (layer, be[i], 0, 0)),
                  pl.BlockSpec((None, None, 1, d), lambda i, be, nb: (layer, be[i], 0, 0))],
        out_specs=pl.BlockSpec((MOE_ROWS, d), lambda i, be, nb: (i, 0)),
        scratch_shapes=[pltpu.VMEM((2, MOE_ROWS, d), F32), pltpu.VMEM((d, 2 * D_FF), BF16),
                        pltpu.VMEM((D_FF, d), BF16), pltpu.SemaphoreType.DMA((2,))],
    )
    return pl.pallas_call(
        _moe_kernel,
        grid_spec=grid_spec,
        out_shape=jax.ShapeDtypeStruct((n_blocks * MOE_ROWS, d), F32),
        compiler_params=_params(("arbitrary",)),
        name="moe_experts",
    )(block_e, n_used, tok3, tok3, h2, w_up, b_up, w_down, b_down)


def _combine_kernel(dst_ref, dstn_ref, x_ref, mod_ref, gate_ref, y_hbm, o_ref, ybuf, sem):
    i = pl.program_id(0)
    n = pl.num_programs(0)
    slot = i % 2

    def gather(dref, dst_slot):
        def body(r, carry):
            for k in range(TOP_K):
                pltpu.make_async_copy(y_hbm.at[pl.ds(dref[0, 0, r * TOP_K + k], 1), :],
                                      ybuf.at[dst_slot, k, pl.ds(r, 1), :], sem.at[dst_slot]).start()
            return carry
        lax.fori_loop(0, ROW_TILE, body, 0)

    @pl.when(i == 0)
    def _():
        gather(dst_ref, 0)

    @pl.when(i + 1 < n)
    def _():
        gather(dstn_ref, 1 - slot)

    for k in range(TOP_K):
        pltpu.make_async_copy(y_hbm.at[pl.ds(0, ROW_TILE), :], ybuf.at[slot, k], sem.at[slot]).wait()
    gate = gate_ref[...]
    f = gate[:, 0:1] * ybuf[slot, 0]
    for k in range(1, TOP_K):
        f = f + gate[:, k:k + 1] * ybuf[slot, k]
    o_ref[...] = x_ref[...] + mod_ref[0, 0][5:6] * f


def _combine_call(dest, x, mod, gate, y, tiles_per_b):
    t, d = x.shape
    n_tiles = t // ROW_TILE
    dst3 = dest.reshape(n_tiles, 1, ROW_TILE * TOP_K)
    return pl.pallas_call(
        _combine_kernel,
        grid=(n_tiles,),
        in_specs=[pl.BlockSpec((1, 1, ROW_TILE * TOP_K), lambda i: (i, 0, 0), memory_space=pltpu.SMEM),
                  pl.BlockSpec((1, 1, ROW_TILE * TOP_K), lambda i: (jnp.minimum(i + 1, n_tiles - 1), 0, 0),
                               memory_space=pltpu.SMEM),
                  pl.BlockSpec((ROW_TILE, d), lambda i: (i, 0)),
                  pl.BlockSpec((1, 1, 6, d), lambda i: (i // tiles_per_b, jnp.minimum(i % tiles_per_b, 1), 0, 0)),
                  pl.BlockSpec((ROW_TILE, LANES), lambda i: (i, 0)),
                  pl.BlockSpec(memory_space=pl.ANY)],
        out_specs=pl.BlockSpec((ROW_TILE, d), lambda i: (i, 0)),
        out_shape=jax.ShapeDtypeStruct((t, d), F32),
        scratch_shapes=[pltpu.VMEM((2, TOP_K, ROW_TILE, d), F32), pltpu.SemaphoreType.DMA((2,))],
        compiler_params=_params(("arbitrary",)),
        name="moe_combine",
    )(dst3, dst3, x, mod, gate, y)


def _final_kernel(x_ref, g_ref, o_ref):
    o_ref[0] = _rms(x_ref[0], g_ref[...])


def _final_call(x3, g, n_ctx):
    bsz, s, d = x3.shape
    skip = n_ctx // ROW_TILE
    return pl.pallas_call(
        _final_kernel,
        grid=(bsz, (s - n_ctx) // ROW_TILE),
        in_specs=[pl.BlockSpec((1, ROW_TILE, d), lambda b, i: (b, i + skip, 0)),
                  pl.BlockSpec((1, d), lambda b, i: (0, 0))],
        out_specs=pl.BlockSpec((1, ROW_TILE, d), lambda b, i: (b, i, 0)),
        out_shape=jax.ShapeDtypeStruct((bsz, s - n_ctx, d), F32),
        compiler_params=_params(("arbitrary", "arbitrary")),
        name="final_norm",
    )(x3, g)


def _rot_cols(w):
    half = MLA_ROPE // 2
    return jnp.concatenate([-w[..., half:], w[..., :half]], axis=-1)


def _pack_weights(w_in, mla_w_uq, mla_w_ukv):
    depth = w_in.shape[0]
    sizes = (MLA_Q_LORA, MLA_KV_LORA, MLA_ROPE, 2 * ML_QK_W, ML_V_W, ML_V_W, 4 * ML_HEADS, SSD_INNER,
             SSD_INNER + 2 * SSD_BC_W, 2 * SSD_HEADS, 3 * D_MODEL)
    idx = [int(v) for v in np.cumsum(sizes)[:-1]]
    w_q, w_kv, w_kr, w_qk, w_v, w_o, w_if, w_z, w_xbc, w_dt, w_g = jnp.split(w_in, idx, axis=-1)
    pad = jnp.zeros(w_in.shape[:2] + (LANES - SM_DT - 2 * SSD_HEADS,), w_in.dtype)
    small = jnp.concatenate([w_kr, _rot_cols(w_kr), w_if, w_dt, pad], axis=-1)
    wp = jnp.concatenate([w_q, w_kv, small, w_xbc, w_qk, w_v, w_o, w_z, w_g], axis=-1).astype(BF16)
    wst = jnp.swapaxes(small, 1, 2).astype(BF16)

    wq = mla_w_uq.reshape(depth, MLA_Q_LORA, MLA_HEADS, MLA_NOPE + MLA_ROPE)
    nope, rope = wq[..., :MLA_NOPE], wq[..., MLA_NOPE:]
    zr = jnp.zeros_like(rope)
    zn = jnp.zeros_like(nope)
    wqm = jnp.concatenate([nope, rope, zr], axis=-1).reshape(depth, MLA_Q_LORA, -1).astype(BF16)
    wqr = jnp.concatenate([zn, _rot_cols(rope), zr], axis=-1).reshape(depth, MLA_Q_LORA, -1).astype(BF16)
    wkv = mla_w_ukv.reshape(depth, MLA_KV_LORA, MLA_HEADS, MLA_NOPE + MLA_V)
    knope, val = wkv[..., :MLA_NOPE], wkv[..., MLA_NOPE:]
    wk = jnp.concatenate([knope, jnp.zeros_like(knope)], axis=-1).reshape(depth, MLA_KV_LORA, -1).astype(BF16)
    wv = val.reshape(depth, MLA_KV_LORA, -1).astype(BF16)
    return wp, wst, wqm, wqr, wk, wv


def _place_matrix():
    e = np.zeros((LANES, MLA_HEADS * HEAD_PAD), np.float32)
    for h in range(MLA_HEADS):
        for jj in range(MLA_ROPE):
            e[SM_KR + jj, h * HEAD_PAD + MLA_NOPE + jj] = 1.0
            e[SM_KROT + jj, h * HEAD_PAD + MLA_NOPE + jj] = 1.0
    return jnp.asarray(e, BF16)


def _rope_tables(n_ctx, n_lat):
    rows = n_lat // GRID_W
    row = jnp.broadcast_to(jnp.arange(rows)[:, None], (rows, GRID_W)).reshape(-1)
    col = jnp.broadcast_to(jnp.arange(GRID_W)[None, :], (rows, GRID_W)).reshape(-1)
    n_freq = MLA_ROPE // 4
    inv = ROPE_BASE ** (-jnp.arange(n_freq, dtype=F32) / n_freq)
    ang = jnp.concatenate([row[:, None] * inv, col[:, None] * inv], axis=-1)
    cos = jnp.concatenate([jnp.ones((n_ctx, MLA_ROPE // 2), F32), jnp.cos(ang)], axis=0)
    sin = jnp.concatenate([jnp.zeros((n_ctx, MLA_ROPE // 2), F32), jnp.sin(ang)], axis=0)
    cs32 = jnp.concatenate([cos, cos], axis=-1)
    sn32 = jnp.concatenate([sin, sin], axis=-1)
    s = n_ctx + n_lat
    z32 = jnp.zeros((s, MLA_ROPE), F32)
    t1 = jnp.concatenate([cs32, sn32, jnp.zeros((s, LANES - 2 * MLA_ROPE), F32)], axis=-1)
    cs = jnp.concatenate([jnp.ones((s, MLA_NOPE), F32), cs32, z32], axis=-1)
    sn = jnp.concatenate([jnp.zeros((s, MLA_NOPE), F32), sn32, z32], axis=-1)
    return t1, cs, sn


def _small_lanes(if_vals, dt_vals):
    v = jnp.zeros((LANES,), F32)
    v = v.at[SM_IF:SM_IF + if_vals.shape[0]].set(if_vals)
    return v.at[SM_DT:SM_DT + dt_vals.shape[0]].set(dt_vals)


def kernel(x, c, ctx, c_ctx, w_ada, b_ada, norm1_g, w_in, mla_qnorm_g, mla_w_uq, mla_kvnorm_g, mla_w_ukv,
           ml_conv_w, ml_conv_b, ml_gate_b, ml_norm_g, ssd_conv_w, ssd_conv_b, ssd_dt_bias, ssd_a_log, ssd_d,
           ssd_norm_g, w_br_mla, w_br_ml, w_br_ssd, w_out, norm2_g, w_router, b_router, w_up, b_up, w_down,
           b_down, final_g):
    bsz, n_lat, d = x.shape
    n_ctx = ctx.shape[1]
    depth = w_in.shape[0]
    s = n_ctx + n_lat
    t = bsz * s
    tiles_per_b = s // ROW_TILE
    assert n_ctx == ROW_TILE and n_lat % ROW_TILE == 0 and d == D_MODEL

    cond = jnp.zeros((16, d), F32).at[:bsz].set(c).at[bsz].set(c_ctx)
    mod_all = _ada_call(cond, w_ada, b_ada)
    mod_lat = mod_all[:, :bsz].reshape(depth, bsz, 1, 6, d)
    mod_ctx = jnp.broadcast_to(mod_all[:, bsz].reshape(depth, 1, 1, 6, d), (depth, bsz, 1, 6, d))
    mod_tab = jnp.concatenate([mod_ctx, mod_lat], axis=2)

    wp, wst, wqm, wqr, wk, wv = _pack_weights(w_in, mla_w_uq, mla_w_ukv)
    emat = _place_matrix()
    t1, cs, sn = _rope_tables(n_ctx, n_lat)
    conv_w = jnp.concatenate([ssd_conv_w, ml_conv_w], axis=-1)
    conv_w = jnp.concatenate([conv_w, jnp.zeros((depth, 8 - CONV_W, CV_W), F32)], axis=1)
    conv_b = jnp.concatenate([ssd_conv_b, ml_conv_b], axis=-1).reshape(depth, 1, CV_W)
    post = jnp.ones((1, CV_W), F32).at[:, CV_K:CV_K + ML_QK_W].set(ML_QK ** -0.5)
    wr = jnp.concatenate([w_router, jnp.zeros((depth, d, LANES - N_EXPERTS), F32)], axis=-1)
    br = jnp.concatenate([b_router, jnp.zeros((depth, LANES - N_EXPERTS), F32)], axis=-1)

    xall = jnp.concatenate([ctx, x], axis=1).reshape(t, d)
    n_rows_max = -(-(t * TOP_K + N_EXPERTS * (MOE_ROWS - 1)) // MOE_ROWS) * MOE_ROWS
    n_blocks = n_rows_max // MOE_ROWS
    tok_ids = jnp.repeat(jnp.arange(t, dtype=jnp.int32), TOP_K)

    for l in range(depth):
        q, k, vv, smc, smt, cv, v, o, z, g = _in_call(
            xall, mod_tab[l], norm1_g[l].reshape(1, d), wp[l], wst[l], mla_qnorm_g[l].reshape(1, -1), wqm[l],
            wqr[l], mla_kvnorm_g[l].reshape(1, -1), wk[l], wv[l], emat, t1, cs, sn, tiles_per_b)
        cvo = _conv_call(cv.reshape(bsz, s, CV_W), conv_w[l], conv_b[l], post, n_ctx)
        att = _attn_call(q.reshape(bsz, s, -1), k.reshape(bsz, s, -1), vv.reshape(bsz, s, -1), n_ctx)
        bias = _small_lanes(ml_gate_b[l].reshape(-1), ssd_dt_bias[l].reshape(-1))
        a_vec = _small_lanes(jnp.zeros((4 * ML_HEADS,), F32), -jnp.exp(ssd_a_log[l].astype(F32)).reshape(-1))
        brow, bcol = bias.reshape(1, LANES), bias.reshape(LANES, 1)
        hf, hb = _mlstm_call(cvo, v.reshape(bsz, s, -1), smc, smt, brow, bcol, n_ctx)
        yf, yb = _ssd_call(cvo, smc, smt, brow, bcol, a_vec.reshape(1, LANES), a_vec.reshape(LANES, 1), n_ctx)
        xall, h2, logits = _out_call(
            xall, mod_tab[l], att.reshape(t, -1), hf.reshape(t, -1), hb.reshape(t, -1), yf.reshape(t, -1),
            yb.reshape(t, -1), cvo.reshape(t, CV_W), z, o, g, ml_norm_g[l].reshape(1, -1),
            jnp.repeat(ssd_d[l], SSD_P).reshape(1, -1), ssd_norm_g[l].reshape(1, -1),
            w_br_mla[l].astype(BF16), w_br_ml[l].astype(BF16), w_br_ssd[l].astype(BF16), w_out[l].astype(BF16),
            norm2_g[l].reshape(1, d), wr[l], br[l].reshape(1, LANES), tiles_per_b)

        e_pad, gate, rank_pad, cnt = _route_call(logits)
        counts = cnt[0, :N_EXPERTS].astype(jnp.int32)
        padded = (counts + MOE_ROWS - 1) // MOE_ROWS * MOE_ROWS
        pad_end = jnp.cumsum(padded)
        pad_start = pad_end - padded
        dest = pad_start[e_pad[:, :TOP_K]] + rank_pad[:, :TOP_K]
        row_tok = jnp.zeros((n_rows_max,), jnp.int32).at[dest.reshape(-1)].set(tok_ids)
        block_e = jnp.minimum(jnp.searchsorted(pad_end, jnp.arange(n_blocks, dtype=jnp.int32) * MOE_ROWS,
                                               side='right'), N_EXPERTS - 1).astype(jnp.int32)
        n_used = (pad_end[-1] // MOE_ROWS).astype(jnp.int32).reshape(1)
        y = _moe_call(block_e, n_used, row_tok, h2, w_up, b_up, w_down, b_down, l)
        xall = _combine_call(dest.astype(jnp.int32), xall, mod_tab[l], gate, y, tiles_per_b)

    return _final_call(xall.reshape(bsz, s, d), final_g.reshape(1, d), n_ctx)
```

```python
import functools
import math

import jax
import jax.numpy as jnp
import numpy as np
from jax import lax
from jax.experimental import pallas as pl
from jax.experimental.pallas import tpu as pltpu

F32 = jnp.float32
BF16 = jnp.bfloat16
HIGHEST = lax.Precision.HIGHEST

D_MODEL = 1024
GRID_W = 64
EPS = 1e-6
ROPE_BASE = 10000.0
CONV_W = 5
MLA_HEADS = 8
MLA_NOPE = 64
MLA_ROPE = 32
MLA_V = 64
MLA_Q_LORA = 384
MLA_KV_LORA = 256
MLA_SCALE = (MLA_NOPE + MLA_ROPE) ** -0.5
LOG2_E = math.log2(math.e)
ML_HEADS = 4
ML_QK = 64
ML_V = 128
ML_QK_W = ML_HEADS * ML_QK
ML_V_W = ML_HEADS * ML_V
SSD_HEADS = 16
SSD_P = 64
SSD_N = 128
SSD_GROUPS = 4
SSD_HPG = SSD_HEADS // SSD_GROUPS
SSD_INNER = SSD_HEADS * SSD_P
SSD_BC_W = SSD_GROUPS * SSD_N
N_EXPERTS = 32
TOP_K = 4
D_FF = 1024
SWIGLU_LIMIT = 7.0
SWIGLU_ALPHA = 1.702

LANES = 128
SUBLANES = 8
HEAD_PAD = 128
ROW_SLABS = D_MODEL // LANES
V_ONES_LANE = (MLA_V, 0)
VMEM_LIMIT = 56 * 1024 * 1024

ROW_TILE = 256
CHUNK = 128
MOE_ROWS = 256

OFF_Q = 0
OFF_KV = OFF_Q + MLA_Q_LORA
OFF_SM = OFF_KV + MLA_KV_LORA
OFF_CV = OFF_SM + LANES
CV_W = 2 * ML_QK_W + SSD_INNER + 2 * SSD_BC_W
OFF_V = OFF_CV + CV_W
OFF_O = OFF_V + ML_V_W
OFF_Z = OFF_O + ML_V_W
OFF_G = OFF_Z + SSD_INNER
IN_PACKED = OFF_G + 3 * D_MODEL
SM_KR = 0
SM_KROT = MLA_ROPE
SM_IF = 2 * MLA_ROPE
SM_DT = SM_IF + 4 * ML_HEADS
CV_X = 0
CV_B = CV_X + SSD_INNER
CV_C = CV_B + SSD_BC_W
CV_Q = CV_C + SSD_BC_W
CV_K = CV_Q + ML_QK_W
CV_SSD_W = CV_Q


def _params(sem, vmem=VMEM_LIMIT):
    return pltpu.CompilerParams(dimension_semantics=sem, vmem_limit_bytes=vmem)


def _dot(a, b):
    return jnp.dot(a, b, preferred_element_type=F32)


def _dot_nt(a, b):
    return lax.dot_general(a, b, (((1,), (1,)), ((), ())), preferred_element_type=F32)


def _dot_hi(a, b):
    return jnp.dot(a, b, preferred_element_type=F32, precision=HIGHEST)


def _sigmoid(x):
    return 0.5 * jnp.tanh(0.5 * x) + 0.5


def _softplus(x):
    return jnp.maximum(x, 0.0) + jnp.log(1.0 + jnp.exp(-jnp.abs(x)))


def _log_sigmoid(x):
    return jnp.minimum(x, 0.0) - jnp.log(1.0 + jnp.exp(-jnp.abs(x)))


def _rms(x, g):
    return x * lax.rsqrt(jnp.mean(x * x, axis=-1, keepdims=True) + EPS) * g


assert ROW_SLABS == SUBLANES


def _to_tiles(ref, val):
    n = val.shape[0]
    for s in range(ROW_SLABS):
        ref[pl.ds(s, n, stride=ROW_SLABS), :] = val[:, s * LANES:(s + 1) * LANES]


def _from_tiles(ref, n):
    return jnp.concatenate([ref[pl.ds(s, n, stride=ROW_SLABS), :] for s in range(ROW_SLABS)], axis=1)


def _ada_kernel(c_ref, w_ref, b_ref, o_ref):
    c = c_ref[...]
    o_ref[0] = _dot_hi(c * _sigmoid(c), w_ref[0]) + b_ref[0]


def _ada_call(cond, w_ada, b_ada):
    depth, d, n = w_ada.shape
    tn = 1536
    return pl.pallas_call(
        _ada_kernel,
        grid=(depth, n // tn),
        in_specs=[pl.BlockSpec((cond.shape[0], d), lambda l, j: (0, 0)),
                  pl.BlockSpec((1, d, tn), lambda l, j: (l, 0, j)),
                  pl.BlockSpec((1, 1, tn), lambda l, j: (l, 0, j))],
        out_specs=pl.BlockSpec((1, cond.shape[0], tn), lambda l, j: (l, 0, j)),
        out_shape=jax.ShapeDtypeStruct((depth, cond.shape[0], n), F32),
        compiler_params=_params(("arbitrary", "arbitrary")),
        name="ada_mod",
    )(cond, w_ada, b_ada.reshape(depth, 1, n))


def _in_kernel(x_ref, mod_ref, g1_ref, w_ref, wst_ref, gq_ref, wqm_ref, wqr_ref, gkv_ref, wk_ref, wv_ref,
               vone_ref, e_ref, t1_ref, cs_ref, sn_ref,
               q_ref, k_ref, vv_ref, smc_ref, smt_ref, cv_ref, v_ref, o_ref, z_ref, g_ref):
    x = x_ref[...]
    mod = mod_ref[0, 0]
    hn = _rms(x, g1_ref[...]) * (1.0 + mod[1:2]) + mod[0:1]
    hb = hn.astype(BF16)

    def proj(lo, hi):
        return _dot(hb, w_ref[:, lo:hi])

    sm = proj(OFF_SM, OFF_CV)
    smc_ref[...] = sm
    smt_ref[...] = _dot_nt(wst_ref[...], hb)
    cv_ref[...] = proj(OFF_CV, OFF_V).astype(BF16)
    v_ref[...] = proj(OFF_V, OFF_O).astype(BF16)
    o_ref[...] = proj(OFF_O, OFF_Z).astype(BF16)
    z_ref[...] = proj(OFF_Z, OFF_G).astype(BF16)
    g_ref[...] = proj(OFF_G, IN_PACKED).astype(BF16)

    qn = _rms(proj(OFF_Q, OFF_KV), gq_ref[...]).astype(BF16)
    qm = _dot(qn, wqm_ref[...])
    qr = _dot(qn, wqr_ref[...])
    cs = cs_ref[...]
    sn = sn_ref[...]
    for h in range(MLA_HEADS):
        sl = slice(h * HEAD_PAD, (h + 1) * HEAD_PAD)
        q_ref[:, sl] = ((qm[:, sl] * cs + qr[:, sl] * sn) * (MLA_SCALE * LOG2_E)).astype(BF16)
    kvn = _rms(proj(OFF_KV, OFF_SM), gkv_ref[...]).astype(BF16)
    kro = (sm * t1_ref[...]).astype(BF16)
    k_ref[...] = (_dot(kvn, wk_ref[...]) + _dot(kro, e_ref[...])).astype(BF16)
    vv_ref[...] = (_dot(kvn, wv_ref[...]) + vone_ref[...]).astype(BF16)


def _in_call(x, mod, g1, wp, wst, gq, wqm, wqr, gkv, wk, wv, vone, emat, t1, cs, sn, tiles_per_b):
    t, d = x.shape
    n_tiles = t // ROW_TILE

    def const(shape):
        return pl.BlockSpec(shape, lambda i: (0,) * len(shape), pipeline_mode=pl.Buffered(1))

    def rows(w):
        return pl.BlockSpec((ROW_TILE, w), lambda i: (i, 0))

    def tab():
        return pl.BlockSpec((ROW_TILE, LANES), lambda i: (i % tiles_per_b, 0))

    widths = [MLA_HEADS * HEAD_PAD, MLA_HEADS * HEAD_PAD, MLA_HEADS * HEAD_PAD, LANES, None, CV_W, ML_V_W, ML_V_W,
              SSD_INNER, 3 * D_MODEL]
    dtypes = [BF16, BF16, BF16, F32, F32, BF16, BF16, BF16, BF16, BF16]
    out_specs, out_shape = [], []
    for w, dt in zip(widths, dtypes):
        if w is None:
            out_specs.append(pl.BlockSpec((LANES, ROW_TILE), lambda i: (0, i)))
            out_shape.append(jax.ShapeDtypeStruct((LANES, t), dt))
        else:
            out_specs.append(rows(w))
            out_shape.append(jax.ShapeDtypeStruct((t, w), dt))
    return pl.pallas_call(
        _in_kernel,
        grid=(n_tiles,),
        in_specs=[rows(d),
                  pl.BlockSpec((1, 1, 6, d), lambda i: (i // tiles_per_b, jnp.minimum(i % tiles_per_b, 1), 0, 0)),
                  const((1, d)), const(wp.shape), const(wst.shape), const((1, MLA_Q_LORA)), const(wqm.shape),
                  const(wqr.shape), const((1, MLA_KV_LORA)), const(wk.shape), const(wv.shape), const(vone.shape),
                  const(emat.shape), tab(), tab(), tab()],
        out_specs=out_specs,
        out_shape=out_shape,
        compiler_params=_params(("arbitrary",)),
        name="in_proj",
    )(x, mod, g1, wp, wst, gq, wqm, wqr, gkv, wk, wv, vone, emat, t1, cs, sn)


def _conv_kernel(x_ref, w_ref, b_ref, s_ref, o_ref, *, n_ctx):
    x = x_ref[0].astype(F32)
    s = x.shape[0]
    t = lax.broadcasted_iota(jnp.int32, x.shape, 0)
    is_ctx = t < n_ctx
    pos = jnp.where(is_ctx, t, t - n_ctx)
    length = jnp.where(is_ctx, n_ctx, s - n_ctx)
    acc = jnp.zeros_like(x) + b_ref[...]
    for j in range(CONV_W):
        d = j - CONV_W // 2
        xs = x if d == 0 else pltpu.roll(x, (-d) % s, 0)
        valid = (pos + d >= 0) & (pos + d < length)
        acc = acc + jnp.where(valid, xs, 0.0) * w_ref[j:j + 1, :]
    o_ref[0] = (acc * _sigmoid(acc) * s_ref[...]).astype(o_ref.dtype)


def _conv_call(cv, w, b, post, n_ctx):
    bsz, s, c = cv.shape
    return pl.pallas_call(
        functools.partial(_conv_kernel, n_ctx=n_ctx),
        grid=(bsz, c // LANES),
        in_specs=[pl.BlockSpec((1, s, LANES), lambda bi, j: (bi, 0, j)),
                  pl.BlockSpec((8, LANES), lambda bi, j: (0, j)),
                  pl.BlockSpec((1, LANES), lambda bi, j: (0, j)),
                  pl.BlockSpec((1, LANES), lambda bi, j: (0, j))],
        out_specs=pl.BlockSpec((1, s, LANES), lambda bi, j: (bi, 0, j)),
        out_shape=jax.ShapeDtypeStruct(cv.shape, BF16),
        compiler_params=_params(("arbitrary", "arbitrary")),
        name="dwconv_silu",
    )(cv, w, b, post)


def _attn_kernel(q_ref, k_ref, v_ref, o_ref, *, n_ctx):
    qi = pl.program_id(2)

    def attend(n_keys):
        outs = []
        for j in range(2):
            sl = slice(j * HEAD_PAD, (j + 1) * HEAD_PAD)
            s = _dot_nt(q_ref[0, :, sl], k_ref[0, 0:n_keys, sl])
            p = jnp.exp2(s - jnp.max(s, axis=-1, keepdims=True))
            pv = _dot(p.astype(BF16), v_ref[0, 0:n_keys, sl])
            ones = V_ONES_LANE[j]
            outs.append(pv / pv[:, ones:ones + 1])
        lane = lax.broadcasted_iota(jnp.int32, outs[0].shape, 1)
        o_ref[0] = jnp.where(lane < MLA_V, outs[0], outs[1]).astype(o_ref.dtype)

    @pl.when(qi == 0)
    def _():
        attend(n_ctx)

    @pl.when(qi != 0)
    def _():
        attend(k_ref.shape[1])


def _attn_call(q, k, v, n_ctx):
    bsz, s, _ = q.shape
    assert n_ctx == ROW_TILE
    pair = 2 * HEAD_PAD
    return pl.pallas_call(
        functools.partial(_attn_kernel, n_ctx=n_ctx),
        grid=(bsz, MLA_HEADS // 2, s // ROW_TILE),
        in_specs=[pl.BlockSpec((1, ROW_TILE, pair), lambda b, h, i: (b, i, h)),
                  pl.BlockSpec((1, s, pair), lambda b, h, i: (b, 0, h)),
                  pl.BlockSpec((1, s, pair), lambda b, h, i: (b, 0, h))],
        out_specs=pl.BlockSpec((1, ROW_TILE, 2 * MLA_V), lambda b, h, i: (b, i, h)),
        out_shape=jax.ShapeDtypeStruct((bsz, s, MLA_HEADS * MLA_V), BF16),
        compiler_params=_params(("arbitrary", "arbitrary", "arbitrary")),
        name="mla_attention",
    )(q, k, v)


def _tri(n, upper):
    r = lax.broadcasted_iota(jnp.int32, (n, n), 0)
    c = lax.broadcasted_iota(jnp.int32, (n, n), 1)
    return (r <= c) if upper else (r >= c)


def _bwd_chunk(j, nc_ctx, nc):
    return jnp.where(j < nc_ctx, nc_ctx - 1 - j, nc + nc_ctx - 1 - j)


def _mlstm_body(qkf_ref, vf_ref, gcf_ref, grf_ref, qkb_ref, vb_ref, gcb_ref, grb_ref, brow_ref, bcol_ref,
                hf_ref, hb_ref, st_ref, m_ref):
    L = CHUNK
    lane = lax.broadcasted_iota(jnp.int32, (L, LANES), 1)
    ones_col = jnp.where(lane == 0, 1.0, 0.0).astype(F32)
    lane_qk = lax.broadcasted_iota(jnp.int32, (L, ML_QK_W), 1)

    def direction(d, qk_ref, v_ref, gc_ref, gr_ref, out_ref):
        rev = d == 1
        gc = gc_ref[...] + brow_ref[...]
        gr = gr_ref[...] + bcol_ref[...]
        fc_all = _dot_hi(jnp.where(_tri(L, rev), 1.0, 0.0).astype(F32), _log_sigmoid(gc))
        fr_all = _dot_hi(_log_sigmoid(gr), jnp.where(_tri(L, not rev), 1.0, 0.0).astype(F32))
        mask = _tri(L, rev)
        end = 0 if rev else L - 1
        q_all = qk_ref[0, :, 0:ML_QK_W]
        k_all = qk_ref[0, :, ML_QK_W:2 * ML_QK_W]
        kt_all = jnp.transpose(k_all.astype(F32)).astype(BF16)
        for h in range(ML_HEADS):
            p = d * ML_HEADS + h
            li_lane = SM_IF + d * 2 * ML_HEADS + h
            lf_lane = li_lane + ML_HEADS
            fc = fc_all[:, lf_lane:lf_lane + 1]
            fr = fr_all[lf_lane:lf_lane + 1, :]
            ic = gc[:, li_lane:li_lane + 1]
            ir = gr[li_lane:li_lane + 1, :]
            last = fc[end:end + 1, :]
            m_old = m_ref[p][0:1, 0:1]
            m_new = jnp.maximum(last + m_old, jnp.max(last - fr + ir, axis=-1, keepdims=True))
            w_src = jnp.exp(last - fc + ic - m_new)
            w_old = jnp.exp(last + m_old - m_new)
            dmat = jnp.where(mask, fc - fr + ir, -jnp.inf)
            g = fc + m_old
            m_row = jnp.maximum(g, jnp.max(dmat, axis=-1, keepdims=True))
            q_h = jnp.where((lane_qk >= h * ML_QK) & (lane_qk < (h + 1) * ML_QK), q_all, 0).astype(BF16)
            s = _dot_nt(q_h, k_all) * jnp.exp(dmat - m_row)
            v_ext = jnp.concatenate([v_ref[0, :, h * ML_V:(h + 1) * ML_V].astype(F32), ones_col], axis=1)
            state = st_ref[d]
            tot = (_dot(s.astype(BF16), v_ext.astype(BF16))
                   + jnp.exp(g - m_row) * _dot(q_h, state.astype(BF16)))
            den = tot[:, ML_V:ML_V + 1]
            out_ref[0, :, h * ML_V:(h + 1) * ML_V] = (
                tot[:, :ML_V] / jnp.maximum(jnp.abs(den), jnp.exp(-m_row))).astype(out_ref.dtype)
            rows = slice(h * ML_QK, (h + 1) * ML_QK)
            st_ref[d, rows, :] = w_old * state[rows, :] + _dot(kt_all[rows, :], (w_src * v_ext).astype(BF16))
            m_ref[p] = jnp.broadcast_to(m_new, m_ref.shape[1:])

    direction(0, qkf_ref, vf_ref, gcf_ref, grf_ref, hf_ref)
    direction(1, qkb_ref, vb_ref, gcb_ref, grb_ref, hb_ref)


def _ssd_body(cvf_ref, gcf_ref, grf_ref, cvb_ref, gcb_ref, grb_ref, brow_ref, bcol_ref, arow_ref, acol_ref,
              yf_ref, yb_ref, st_ref):
    L = CHUNK
    gw = SSD_HPG * SSD_P
    lane_g = lax.broadcasted_iota(jnp.int32, (L, gw), 1)

    def per_head(cols):
        out = cols[SSD_HPG - 1]
        for i in range(SSD_HPG - 2, -1, -1):
            out = jnp.where(lane_g[0:cols[0].shape[0]] < (i + 1) * SSD_P, cols[i], out)
        return out

    def direction(d, cv_ref, gc_ref, gr_ref, out_ref):
        rev = d == 1
        dtc = _softplus(gc_ref[...] + brow_ref[...])
        dtr = _softplus(gr_ref[...] + bcol_ref[...])
        ac_all = _dot_hi(jnp.where(_tri(L, rev), 1.0, 0.0).astype(F32), dtc * arow_ref[...])
        ar_all = _dot_hi(dtr * acol_ref[...], jnp.where(_tri(L, not rev), 1.0, 0.0).astype(F32))
        mask = _tri(L, rev)
        end = 0 if rev else L - 1
        for gi in range(SSD_GROUPS):
            x_g = cv_ref[0, :, CV_X + gi * gw:CV_X + (gi + 1) * gw]
            b_g = cv_ref[0, :, CV_B + gi * SSD_N:CV_B + (gi + 1) * SSD_N]
            c_g = cv_ref[0, :, CV_C + gi * SSD_N:CV_C + (gi + 1) * SSD_N]
            cb = _dot_nt(c_g, b_g)
            state = st_ref[d, gi]
            y = jnp.zeros((L, gw), F32)
            e_cols, w_cols, last_cols = [], [], []
            for hg in range(SSD_HPG):
                ln = SM_DT + d * SSD_HEADS + gi * SSD_HPG + hg
                ac = ac_all[:, ln:ln + 1]
                ar = ar_all[ln:ln + 1, :]
                last = ac[end:end + 1, :]
                seg = jnp.exp(jnp.where(mask, ac - ar, -jnp.inf))
                mm = (seg * cb * dtr[ln:ln + 1, :]).astype(BF16)
                y = jnp.where((lane_g >= hg * SSD_P) & (lane_g < (hg + 1) * SSD_P), _dot(mm, x_g), y)
                e_cols.append(jnp.exp(ac))
                w_cols.append(jnp.exp(last - ac) * dtc[:, ln:ln + 1])
                last_cols.append(jnp.exp(last))
            y = y + _dot(c_g, state.astype(BF16)) * per_head(e_cols)
            out_ref[0, :, gi * gw:(gi + 1) * gw] = y.astype(out_ref.dtype)
            bt = jnp.transpose(b_g.astype(F32)).astype(BF16)
            xw = (x_g.astype(F32) * per_head(w_cols)).astype(BF16)
            st_ref[d, gi] = per_head(last_cols) * state + _dot(bt, xw)

    direction(0, cvf_ref, gcf_ref, grf_ref, yf_ref)
    direction(1, cvb_ref, gcb_ref, grb_ref, yb_ref)


def _scan_kernel(qkf_ref, vf_ref, cvf_ref, gcf_ref, grf_ref, qkb_ref, vb_ref, cvb_ref, gcb_ref, grb_ref,
                 brow_ref, bcol_ref, arow_ref, acol_ref, hf_ref, hb_ref, yf_ref, yb_ref, mst_ref, m_ref, sst_ref):
    @pl.when(pl.program_id(1) == 0)
    def _():
        mst_ref[...] = jnp.zeros_like(mst_ref)
        m_ref[...] = jnp.zeros_like(m_ref)
        sst_ref[...] = jnp.zeros_like(sst_ref)

    _mlstm_body(qkf_ref, vf_ref, gcf_ref, grf_ref, qkb_ref, vb_ref, gcb_ref, grb_ref, brow_ref, bcol_ref,
                hf_ref, hb_ref, mst_ref, m_ref)
    _ssd_body(cvf_ref, gcf_ref, grf_ref, cvb_ref, gcb_ref, grb_ref, brow_ref, bcol_ref, arow_ref, acol_ref,
              yf_ref, yb_ref, sst_ref)


def _scan_call(cvo, v, smc, smt, brow, bcol, arow, acol, n_ctx):
    bsz, s, _ = cvo.shape
    nc, nc_ctx = s // CHUNK, n_ctx // CHUNK
    qk_w = 2 * ML_QK_W

    def fwd(b, j):
        return j

    def bwd(b, j):
        return _bwd_chunk(j, nc_ctx, nc)

    def specs(ch):
        return [pl.BlockSpec((1, CHUNK, qk_w), lambda b, j: (b, ch(b, j), CV_Q // qk_w)),
                pl.BlockSpec((1, CHUNK, ML_V_W), lambda b, j: (b, ch(b, j), 0)),
                pl.BlockSpec((1, CHUNK, CV_SSD_W), lambda b, j: (b, ch(b, j), 0)),
                pl.BlockSpec((CHUNK, LANES), lambda b, j: (b * nc + ch(b, j), 0)),
                pl.BlockSpec((LANES, CHUNK), lambda b, j: (0, b * nc + ch(b, j)))]

    def const(shape):
        return pl.BlockSpec(shape, lambda b, j: (0, 0))

    def out_spec(ch, w):
        return pl.BlockSpec((1, CHUNK, w), lambda b, j: (b, ch(b, j), 0))

    h_out = jax.ShapeDtypeStruct((bsz, s, ML_V_W), F32)
    y_out = jax.ShapeDtypeStruct((bsz, s, SSD_INNER), F32)
    return pl.pallas_call(
        _scan_kernel,
        grid=(bsz, nc),
        in_specs=specs(fwd) + specs(bwd) + [const((1, LANES)), const((LANES, 1)), const((1, LANES)),
                                            const((LANES, 1))],
        out_specs=[out_spec(fwd, ML_V_W), out_spec(bwd, ML_V_W), out_spec(fwd, SSD_INNER),
                   out_spec(bwd, SSD_INNER)],
        out_shape=[h_out, h_out, y_out, y_out],
        scratch_shapes=[pltpu.VMEM((2, ML_QK_W, ML_V + LANES), F32),
                        pltpu.VMEM((2 * ML_HEADS, 8, LANES), F32),
                        pltpu.VMEM((2, SSD_GROUPS, SSD_N, SSD_HPG * SSD_P), F32)],
        compiler_params=_params(("arbitrary", "arbitrary")),
        name="mixer_scan",
    )(cvo, v, cvo, smc, smt, cvo, v, cvo, smc, smt, brow, bcol, arow, acol)


def _out_kernel(x_ref, mod_ref, att_ref, hf_ref, hb_ref, yf_ref, yb_ref, sx_ref, z_ref, o_ref, g_ref,
                mlg_ref, sd_ref, sg_ref, wa_ref, wm_ref, ws_ref, wo_ref, g2_ref, wr_ref, br_ref,
                xo_ref, h2_ref, lg_ref):
    mod = mod_ref[0, 0]
    hm = hf_ref[...] + hb_ref[...]
    m_parts = []
    for h in range(ML_HEADS):
        sl = slice(h * ML_V, (h + 1) * ML_V)
        m_parts.append(_rms(hm[:, sl], mlg_ref[:, sl]))
    m_out = jnp.concatenate(m_parts, axis=1) * _sigmoid(o_ref[...].astype(F32))
    z = z_ref[...].astype(F32)
    y = (yf_ref[...] + yb_ref[...] + sd_ref[...] * sx_ref[...].astype(F32)) * (z * _sigmoid(z))
    gw = SSD_HPG * SSD_P
    s_parts = []
    for gi in range(SSD_GROUPS):
        sl = slice(gi * gw, (gi + 1) * gw)
        s_parts.append(_rms(y[:, sl], sg_ref[:, sl]))
    s_out = jnp.concatenate(s_parts, axis=1)
    gt = g_ref[...].astype(F32)
    d = D_MODEL
    merged = (_sigmoid(gt[:, 0:d]) * _dot(att_ref[...], wa_ref[...])
              + _sigmoid(gt[:, d:2 * d]) * _dot(m_out.astype(BF16), wm_ref[...])
              + _sigmoid(gt[:, 2 * d:3 * d]) * _dot(s_out.astype(BF16), ws_ref[...]))
    xn = x_ref[...] + mod[2:3] * _dot(merged.astype(BF16), wo_ref[...])
    xo_ref[...] = xn
    h2 = _rms(xn, g2_ref[...]) * (1.0 + mod[4:5]) + mod[3:4]
    _to_tiles(h2_ref, h2)
    lg_ref[...] = _dot_hi(h2, wr_ref[...]) + br_ref[...]


def _out_call(x, mod, att, hf, hb, yf, yb, cvo, z, o, g, mlg, sd, sg, wa, wm, ws, wo, g2, wr, br, tiles_per_b):
    t, d = x.shape
    n_tiles = t // ROW_TILE

    def const(shape):
        return pl.BlockSpec(shape, lambda i: (0,) * len(shape), pipeline_mode=pl.Buffered(1))

    def rows(w, blk=0):
        return pl.BlockSpec((ROW_TILE, w), lambda i: (i, blk))

    return pl.pallas_call(
        _out_kernel,
        grid=(n_tiles,),
        in_specs=[rows(d),
                  pl.BlockSpec((1, 1, 6, d), lambda i: (i // tiles_per_b, jnp.minimum(i % tiles_per_b, 1), 0, 0)),
                  rows(MLA_HEADS * MLA_V), rows(ML_V_W), rows(ML_V_W), rows(SSD_INNER), rows(SSD_INNER),
                  rows(SSD_INNER, CV_X // SSD_INNER), rows(SSD_INNER), rows(ML_V_W), rows(3 * d),
                  const((1, ML_V_W)), const((1, SSD_INNER)), const((1, SSD_INNER)),
                  const(wa.shape), const(wm.shape), const(ws.shape), const(wo.shape), const((1, d)),
                  const(wr.shape), const((1, LANES))],
        out_specs=[rows(d), pl.BlockSpec((ROW_TILE * ROW_SLABS, LANES), lambda i: (i, 0)), rows(LANES)],
        out_shape=[jax.ShapeDtypeStruct((t, d), F32), jax.ShapeDtypeStruct((t * ROW_SLABS, LANES), F32),
                   jax.ShapeDtypeStruct((t, LANES), F32)],
        compiler_params=_params(("arbitrary",)),
        name="mix_out",
    )(x, mod, att, hf, hb, yf, yb, cvo, z, o, g, mlg, sd, sg, wa, wm, ws, wo, g2, wr, br)


def _route_kernel(lg_ref, e_ref, gate_ref, rank_ref, cnt_ref, base_ref):
    i = pl.program_id(0)

    @pl.when(i == 0)
    def _():
        base_ref[...] = jnp.zeros_like(base_ref)

    v = lg_ref[...]
    n = v.shape[0]
    lane = lax.broadcasted_iota(jnp.int32, v.shape, 1)
    v = jnp.where(lane < N_EXPERTS, v, -jnp.inf)
    tops, hots = [], []
    e_out = jnp.zeros(v.shape, jnp.int32)
    for k in range(TOP_K):
        mk = jnp.max(v, axis=-1, keepdims=True)
        idx = jnp.min(jnp.where(v == mk, lane, LANES), axis=-1, keepdims=True)
        hot = lane == idx
        v = jnp.where(hot, -jnp.inf, v)
        tops.append(mk)
        hots.append(hot)
        e_out = jnp.where(lane == k, idx, e_out)
    ex = [jnp.exp(tk - tops[0]) for tk in tops]
    den = ex[0]
    for k in range(1, TOP_K):
        den = den + ex[k]
    gate = jnp.zeros(v.shape, F32)
    for k in range(TOP_K):
        gate = jnp.where(lane == k, ex[k] / den, gate)
    hot_sum = jnp.zeros(v.shape, F32)
    for k in range(TOP_K):
        hot_sum = hot_sum + jnp.where(hots[k], 1.0, 0.0)
    r = lax.broadcasted_iota(jnp.int32, (n, n), 0)
    c = lax.broadcasted_iota(jnp.int32, (n, n), 1)
    before = jnp.where(c < r, 1.0, 0.0).astype(BF16)
    base = base_ref[0:1, :]
    cum = _dot(before, hot_sum.astype(BF16)) + base
    rank = jnp.zeros(v.shape, jnp.int32)
    for k in range(TOP_K):
        rk = jnp.sum(jnp.where(hots[k], cum, 0.0), axis=-1, keepdims=True)
        rank = jnp.where(lane == k, rk.astype(jnp.int32), rank)
    e_ref[...] = e_out
    gate_ref[...] = gate
    rank_ref[...] = rank
    new_base = base + jnp.sum(hot_sum, axis=0, keepdims=True)
    base_ref[...] = jnp.broadcast_to(new_base, base_ref.shape)
    cnt_ref[...] = jnp.broadcast_to(new_base, cnt_ref.shape)


def _route_call(logits):
    t = logits.shape[0]

    def rows():
        return pl.BlockSpec((ROW_TILE, LANES), lambda i: (i, 0))

    return pl.pallas_call(
        _route_kernel,
        grid=(t // ROW_TILE,),
        in_specs=[rows()],
        out_specs=[rows(), rows(), rows(), pl.BlockSpec((8, LANES), lambda i: (0, 0))],
        out_shape=[jax.ShapeDtypeStruct((t, LANES), jnp.int32), jax.ShapeDtypeStruct((t, LANES), F32),
                   jax.ShapeDtypeStruct((t, LANES), jnp.int32), jax.ShapeDtypeStruct((8, LANES), F32)],
        scratch_shapes=[pltpu.VMEM((8, LANES), F32)],
        compiler_params=_params(("arbitrary",)),
        name="moe_route",
    )(logits)


def _dest_kernel(e_ref, rank_ref, start_ref, d_ref):
    e = e_ref[...]
    rank = rank_ref[...]
    lane = lax.broadcasted_iota(jnp.int32, e.shape, 1)
    out = jnp.zeros(e.shape, jnp.int32)
    for k in range(TOP_K):
        base = jnp.sum(jnp.where(lane == e[:, k:k + 1], start_ref[...], 0), axis=-1, keepdims=True)
        out = jnp.where(lane == k, (base + rank[:, k:k + 1]) * SUBLANES, out)
    d_ref[...] = out


def _dest_call(e_pad, rank_pad, start_row):
    t = e_pad.shape[0]

    def rows():
        return pl.BlockSpec((ROW_TILE, LANES), lambda i: (i, 0))

    return pl.pallas_call(
        _dest_kernel,
        grid=(t // ROW_TILE,),
        in_specs=[rows(), rows(), pl.BlockSpec((1, LANES), lambda i: (0, 0))],
        out_specs=rows(),
        out_shape=jax.ShapeDtypeStruct((t, LANES), jnp.int32),
        compiler_params=_params(("arbitrary",)),
        name="moe_dest",
    )(e_pad, rank_pad, start_row)


def _dispatch_kernel(lo_ref, hi_ref, dst_ref, h_hbm, xs_hbm, zrow, sem, zsem):
    i = pl.program_id(0)
    n = pl.num_programs(0)
    slot = i % 2
    rows_per_step = ROW_TILE * TOP_K

    def tile(ref, first):
        return ref.at[pl.ds(pl.multiple_of(first, SUBLANES), SUBLANES), :]

    def zero_copy(r):
        return pltpu.make_async_copy(zrow, tile(xs_hbm, r * SUBLANES), zsem)

    @pl.when(i == 0)
    def _():
        zrow[...] = jnp.zeros_like(zrow)
        for e in range(N_EXPERTS + 1):
            def start(r, carry):
                zero_copy(r).start()
                return carry
            lax.fori_loop(lo_ref[e], hi_ref[e], start, 0)
        for e in range(N_EXPERTS + 1):
            def wait(r, carry):
                zero_copy(r).wait()
                return carry
            lax.fori_loop(lo_ref[e], hi_ref[e], wait, 0)

    def body(r, carry):
        src = tile(h_hbm, (i * ROW_TILE + r) * SUBLANES)
        for k in range(TOP_K):
            pltpu.make_async_copy(src, tile(xs_hbm, dst_ref[0, 0, r * TOP_K + k]), sem.at[slot]).start()
        return carry
    lax.fori_loop(0, ROW_TILE, body, 0)

    def wait_step(s):
        n_sub = rows_per_step * SUBLANES
        pltpu.make_async_copy(h_hbm.at[pl.ds(0, n_sub), :], xs_hbm.at[pl.ds(0, n_sub), :], sem.at[s]).wait()

    @pl.when(i > 0)
    def _():
        wait_step(1 - slot)

    @pl.when(i == n - 1)
    def _():
        wait_step(slot)


def _dispatch_call(fill_lo, fill_hi, dst3, h2, n_rows):
    n_tiles = dst3.shape[0]
    grid_spec = pltpu.PrefetchScalarGridSpec(
        num_scalar_prefetch=2,
        grid=(n_tiles,),
        in_specs=[pl.BlockSpec((1, 1, ROW_TILE * TOP_K), lambda i, lo, hi: (i, 0, 0), memory_space=pltpu.SMEM),
                  pl.BlockSpec(memory_space=pl.ANY)],
        out_specs=pl.BlockSpec(memory_space=pl.ANY),
        scratch_shapes=[pltpu.VMEM((SUBLANES, LANES), F32), pltpu.SemaphoreType.DMA((2,)),
                        pltpu.SemaphoreType.DMA(())],
    )
    return pl.pallas_call(
        _dispatch_kernel,
        grid_spec=grid_spec,
        out_shape=jax.ShapeDtypeStruct((n_rows * ROW_SLABS, LANES), F32),
        compiler_params=_params(("arbitrary",)),
        name="moe_dispatch",
    )(fill_lo, fill_hi, dst3, h2)


def _moe_kernel(be_ref, nb_ref, x_ref, wu_ref, bu_ref, wd_ref, bd_ref, y_ref, wub, wdb):
    i = pl.program_id(0)

    @pl.when(i < nb_ref[0])
    def _():
        prev = be_ref[jnp.maximum(i - 1, 0)]

        @pl.when((i == 0) | (be_ref[i] != prev))
        def _():
            wub[...] = wu_ref[...].astype(BF16)
            wdb[...] = wd_ref[...].astype(BF16)

        gu = _dot(_from_tiles(x_ref, MOE_ROWS).astype(BF16), wub[...]) + bu_ref[...]
        glu = jnp.minimum(gu[:, :D_FF], SWIGLU_LIMIT)
        lin = jnp.clip(gu[:, D_FF:], -SWIGLU_LIMIT, SWIGLU_LIMIT)
        act = glu * _sigmoid(SWIGLU_ALPHA * glu) * (lin + 1.0)
        _to_tiles(y_ref, _dot(act.astype(BF16), wdb[...]) + bd_ref[...])

    @pl.when(i >= nb_ref[0])
    def _():
        y_ref[...] = jnp.zeros_like(y_ref)


def _moe_call(block_e, n_used, xs, w_up, b_up, w_down, b_down, layer):
    n_blocks = block_e.shape[0]
    d = D_MODEL
    blk_rows = MOE_ROWS * ROW_SLABS
    b_up = b_up.reshape(b_up.shape[0], N_EXPERTS, 1, 2 * D_FF)
    b_down = b_down.reshape(b_down.shape[0], N_EXPERTS, 1, d)

    def blk(i, nb):
        return jnp.minimum(i, nb[0] - 1)

    def wspec(shape):
        return pl.BlockSpec((None, None) + shape, lambda i, be, nb: (layer, be[blk(i, nb)], 0, 0))

    grid_spec = pltpu.PrefetchScalarGridSpec(
        num_scalar_prefetch=2,
        grid=(n_blocks,),
        in_specs=[pl.BlockSpec((blk_rows, LANES), lambda i, be, nb: (blk(i, nb), 0)),
                  wspec((d, 2 * D_FF)), wspec((1, 2 * D_FF)), wspec((D_FF, d)), wspec((1, d))],
        out_specs=pl.BlockSpec((blk_rows, LANES), lambda i, be, nb: (i, 0)),
        scratch_shapes=[pltpu.VMEM((d, 2 * D_FF), BF16), pltpu.VMEM((D_FF, d), BF16)],
    )
    return pl.pallas_call(
        _moe_kernel,
        grid_spec=grid_spec,
        out_shape=jax.ShapeDtypeStruct((n_blocks * blk_rows, LANES), F32),
        compiler_params=_params(("arbitrary",)),
        name="moe_experts",
    )(block_e, n_used, xs, w_up, b_up, w_down, b_down)


def _combine_kernel(dst_ref, dstn_ref, x_ref, mod_ref, gate_ref, y_hbm, o_ref, ybuf, sem):
    i = pl.program_id(0)
    n = pl.num_programs(0)
    slot = i % 2

    def gather(dref, dst_slot):
        def body(r, carry):
            for k in range(TOP_K):
                first = pl.multiple_of(dref[0, 0, r * TOP_K + k], SUBLANES)
                pltpu.make_async_copy(y_hbm.at[pl.ds(first, SUBLANES), :],
                                      ybuf.at[dst_slot, k, pl.ds(pl.multiple_of(r * SUBLANES, SUBLANES), SUBLANES), :],
                                      sem.at[dst_slot]).start()
            return carry
        lax.fori_loop(0, ROW_TILE, body, 0)

    @pl.when(i == 0)
    def _():
        gather(dst_ref, 0)

    @pl.when(i + 1 < n)
    def _():
        gather(dstn_ref, 1 - slot)

    for k in range(TOP_K):
        pltpu.make_async_copy(y_hbm.at[pl.ds(0, ROW_TILE * SUBLANES), :], ybuf.at[slot, k], sem.at[slot]).wait()
    gate = gate_ref[...]
    f = gate[:, 0:1] * _from_tiles(ybuf.at[slot, 0], ROW_TILE)
    for k in range(1, TOP_K):
        f = f + gate[:, k:k + 1] * _from_tiles(ybuf.at[slot, k], ROW_TILE)
    o_ref[...] = x_ref[...] + mod_ref[0, 0][5:6] * f


def _combine_call(dst3, x, mod, gate, y, tiles_per_b):
    t, d = x.shape
    n_tiles = t // ROW_TILE
    return pl.pallas_call(
        _combine_kernel,
        grid=(n_tiles,),
        in_specs=[pl.BlockSpec((1, 1, ROW_TILE * TOP_K), lambda i: (i, 0, 0), memory_space=pltpu.SMEM),
                  pl.BlockSpec((1, 1, ROW_TILE * TOP_K), lambda i: (jnp.minimum(i + 1, n_tiles - 1), 0, 0),
                               memory_space=pltpu.SMEM),
                  pl.BlockSpec((ROW_TILE, d), lambda i: (i, 0)),
                  pl.BlockSpec((1, 1, 6, d), lambda i: (i // tiles_per_b, jnp.minimum(i % tiles_per_b, 1), 0, 0)),
                  pl.BlockSpec((ROW_TILE, LANES), lambda i: (i, 0)),
                  pl.BlockSpec(memory_space=pl.ANY)],
        out_specs=pl.BlockSpec((ROW_TILE, d), lambda i: (i, 0)),
        out_shape=jax.ShapeDtypeStruct((t, d), F32),
        scratch_shapes=[pltpu.VMEM((2, TOP_K, ROW_TILE * ROW_SLABS, LANES), F32),
                        pltpu.SemaphoreType.DMA((2,))],
        compiler_params=_params(("arbitrary",)),
        name="moe_combine",
    )(dst3, dst3, x, mod, gate, y)


def _final_kernel(x_ref, g_ref, o_ref):
    o_ref[0] = _rms(x_ref[0], g_ref[...])


def _final_call(x3, g, n_ctx):
    bsz, s, d = x3.shape
    skip = n_ctx // ROW_TILE
    return pl.pallas_call(
        _final_kernel,
        grid=(bsz, (s - n_ctx) // ROW_TILE),
        in_specs=[pl.BlockSpec((1, ROW_TILE, d), lambda b, i: (b, i + skip, 0)),
                  pl.BlockSpec((1, d), lambda b, i: (0, 0))],
        out_specs=pl.BlockSpec((1, ROW_TILE, d), lambda b, i: (b, i, 0)),
        out_shape=jax.ShapeDtypeStruct((bsz, s - n_ctx, d), F32),
        compiler_params=_params(("arbitrary", "arbitrary")),
        name="final_norm",
    )(x3, g)


def _rot_cols(w):
    half = MLA_ROPE // 2
    return jnp.concatenate([-w[..., half:], w[..., :half]], axis=-1)


def _pack_weights(w_in, mla_w_uq, mla_w_ukv):
    depth = w_in.shape[0]
    sizes = (MLA_Q_LORA, MLA_KV_LORA, MLA_ROPE, 2 * ML_QK_W, ML_V_W, ML_V_W, 4 * ML_HEADS, SSD_INNER,
             SSD_INNER + 2 * SSD_BC_W, 2 * SSD_HEADS, 3 * D_MODEL)
    idx = [int(v) for v in np.cumsum(sizes)[:-1]]
    w_q, w_kv, w_kr, w_qk, w_v, w_o, w_if, w_z, w_xbc, w_dt, w_g = jnp.split(w_in, idx, axis=-1)
    pad = jnp.zeros(w_in.shape[:2] + (LANES - SM_DT - 2 * SSD_HEADS,), w_in.dtype)
    small = jnp.concatenate([w_kr, _rot_cols(w_kr), w_if, w_dt, pad], axis=-1)
    wp = jnp.concatenate([w_q, w_kv, small, w_xbc, w_qk, w_v, w_o, w_z, w_g], axis=-1).astype(BF16)
    wst = jnp.swapaxes(small, 1, 2).astype(BF16)

    wq = mla_w_uq.reshape(depth, MLA_Q_LORA, MLA_HEADS, MLA_NOPE + MLA_ROPE)
    nope, rope = wq[..., :MLA_NOPE], wq[..., MLA_NOPE:]
    zr = jnp.zeros_like(rope)
    zn = jnp.zeros_like(nope)
    wqm = jnp.concatenate([nope, rope, zr], axis=-1).reshape(depth, MLA_Q_LORA, -1).astype(BF16)
    wqr = jnp.concatenate([zn, _rot_cols(rope), zr], axis=-1).reshape(depth, MLA_Q_LORA, -1).astype(BF16)
    wkv = mla_w_ukv.reshape(depth, MLA_KV_LORA, MLA_HEADS, MLA_NOPE + MLA_V)
    knope, val = wkv[..., :MLA_NOPE], wkv[..., MLA_NOPE:]
    wk = jnp.concatenate([knope, jnp.zeros_like(knope)], axis=-1).reshape(depth, MLA_KV_LORA, -1).astype(BF16)
    zv = jnp.zeros_like(val)
    val_even = jnp.concatenate([val, zv], axis=-1)[:, :, 0::2]
    val_odd = jnp.concatenate([zv, val], axis=-1)[:, :, 1::2]
    wv = jnp.stack([val_even, val_odd], axis=3).reshape(depth, MLA_KV_LORA, -1).astype(BF16)
    return wp, wst, wqm, wqr, wk, wv


def _value_ones_row():
    e = np.zeros((1, MLA_HEADS * HEAD_PAD), np.float32)
    for h in range(MLA_HEADS):
        e[0, h * HEAD_PAD + V_ONES_LANE[h % 2]] = 1.0
    return jnp.asarray(e)


def _place_matrix():
    e = np.zeros((LANES, MLA_HEADS * HEAD_PAD), np.float32)
    for h in range(MLA_HEADS):
        for jj in range(MLA_ROPE):
            e[SM_KR + jj, h * HEAD_PAD + MLA_NOPE + jj] = 1.0
            e[SM_KROT + jj, h * HEAD_PAD + MLA_NOPE + jj] = 1.0
    return jnp.asarray(e, BF16)


def _rope_tables(n_ctx, n_lat):
    rows = n_lat // GRID_W
    row = jnp.broadcast_to(jnp.arange(rows)[:, None], (rows, GRID_W)).reshape(-1)
    col = jnp.broadcast_to(jnp.arange(GRID_W)[None, :], (rows, GRID_W)).reshape(-1)
    n_freq = MLA_ROPE // 4
    inv = ROPE_BASE ** (-jnp.arange(n_freq, dtype=F32) / n_freq)
    ang = jnp.concatenate([row[:, None] * inv, col[:, None] * inv], axis=-1)
    cos = jnp.concatenate([jnp.ones((n_ctx, MLA_ROPE // 2), F32), jnp.cos(ang)], axis=0)
    sin = jnp.concatenate([jnp.zeros((n_ctx, MLA_ROPE // 2), F32), jnp.sin(ang)], axis=0)
    cs32 = jnp.concatenate([cos, cos], axis=-1)
    sn32 = jnp.concatenate([sin, sin], axis=-1)
    s = n_ctx + n_lat
    z32 = jnp.zeros((s, MLA_ROPE), F32)
    t1 = jnp.concatenate([cs32, sn32, jnp.zeros((s, LANES - 2 * MLA_ROPE), F32)], axis=-1)
    cs = jnp.concatenate([jnp.ones((s, MLA_NOPE), F32), cs32, z32], axis=-1)
    sn = jnp.concatenate([jnp.zeros((s, MLA_NOPE), F32), sn32, z32], axis=-1)
    return t1, cs, sn


def _small_lanes(if_vals, dt_vals):
    v = jnp.zeros((LANES,), F32)
    v = v.at[SM_IF:SM_IF + if_vals.shape[0]].set(if_vals)
    return v.at[SM_DT:SM_DT + dt_vals.shape[0]].set(dt_vals)


def kernel(x, c, ctx, c_ctx, w_ada, b_ada, norm1_g, w_in, mla_qnorm_g, mla_w_uq, mla_kvnorm_g, mla_w_ukv,
           ml_conv_w, ml_conv_b, ml_gate_b, ml_norm_g, ssd_conv_w, ssd_conv_b, ssd_dt_bias, ssd_a_log, ssd_d,
           ssd_norm_g, w_br_mla, w_br_ml, w_br_ssd, w_out, norm2_g, w_router, b_router, w_up, b_up, w_down,
           b_down, final_g):
    bsz, n_lat, d = x.shape
    n_ctx = ctx.shape[1]
    depth = w_in.shape[0]
    s = n_ctx + n_lat
    t = bsz * s
    tiles_per_b = s // ROW_TILE
    assert n_ctx == ROW_TILE and n_lat % ROW_TILE == 0 and d == D_MODEL

    cond = jnp.zeros((16, d), F32).at[:bsz].set(c).at[bsz].set(c_ctx)
    mod_all = _ada_call(cond, w_ada, b_ada)
    mod_lat = mod_all[:, :bsz].reshape(depth, bsz, 1, 6, d)
    mod_ctx = jnp.broadcast_to(mod_all[:, bsz].reshape(depth, 1, 1, 6, d), (depth, bsz, 1, 6, d))
    mod_tab = jnp.concatenate([mod_ctx, mod_lat], axis=2)

    wp, wst, wqm, wqr, wk, wv = _pack_weights(w_in, mla_w_uq, mla_w_ukv)
    emat = _place_matrix()
    vone = _value_ones_row()
    t1, cs, sn = _rope_tables(n_ctx, n_lat)
    conv_w = jnp.concatenate([ssd_conv_w, ml_conv_w], axis=-1)
    conv_w = jnp.concatenate([conv_w, jnp.zeros((depth, 8 - CONV_W, CV_W), F32)], axis=1)
    conv_b = jnp.concatenate([ssd_conv_b, ml_conv_b], axis=-1).reshape(depth, 1, CV_W)
    post = jnp.ones((1, CV_W), F32).at[:, CV_K:CV_K + ML_QK_W].set(ML_QK ** -0.5)
    wr = jnp.concatenate([w_router, jnp.zeros((depth, d, LANES - N_EXPERTS), F32)], axis=-1)
    br = jnp.concatenate([b_router, jnp.zeros((depth, LANES - N_EXPERTS), F32)], axis=-1)

    xall = jnp.concatenate([ctx, x], axis=1).reshape(t, d)
    n_rows_max = -(-(t * TOP_K + N_EXPERTS * (MOE_ROWS - 1)) // MOE_ROWS) * MOE_ROWS
    n_blocks = n_rows_max // MOE_ROWS

    for l in range(depth):
        q, k, vv, smc, smt, cv, v, o, z, g = _in_call(
            xall, mod_tab[l], norm1_g[l].reshape(1, d), wp[l], wst[l], mla_qnorm_g[l].reshape(1, -1), wqm[l],
            wqr[l], mla_kvnorm_g[l].reshape(1, -1), wk[l], wv[l], vone, emat, t1, cs, sn, tiles_per_b)
        cvo = _conv_call(cv.reshape(bsz, s, CV_W), conv_w[l], conv_b[l], post, n_ctx)
        att = _attn_call(q.reshape(bsz, s, -1), k.reshape(bsz, s, -1), vv.reshape(bsz, s, -1), n_ctx)
        bias = _small_lanes(ml_gate_b[l].reshape(-1), ssd_dt_bias[l].reshape(-1))
        a_vec = _small_lanes(jnp.zeros((4 * ML_HEADS,), F32), -jnp.exp(ssd_a_log[l].astype(F32)).reshape(-1))
        brow, bcol = bias.reshape(1, LANES), bias.reshape(LANES, 1)
        hf, hb, yf, yb = _scan_call(cvo, v.reshape(bsz, s, -1), smc, smt, brow, bcol, a_vec.reshape(1, LANES),
                                    a_vec.reshape(LANES, 1), n_ctx)
        xall, h2, logits = _out_call(
            xall, mod_tab[l], att.reshape(t, -1), hf.reshape(t, -1), hb.reshape(t, -1), yf.reshape(t, -1),
            yb.reshape(t, -1), cvo.reshape(t, CV_W), z, o, g, ml_norm_g[l].reshape(1, -1),
            jnp.repeat(ssd_d[l], SSD_P).reshape(1, -1), ssd_norm_g[l].reshape(1, -1),
            w_br_mla[l].astype(BF16), w_br_ml[l].astype(BF16), w_br_ssd[l].astype(BF16), w_out[l].astype(BF16),
            norm2_g[l].reshape(1, d), wr[l], br[l].reshape(1, LANES), tiles_per_b)

        e_pad, gate, rank_pad, cnt = _route_call(logits)
        counts = cnt[0, :N_EXPERTS].astype(jnp.int32)
        padded = (counts + MOE_ROWS - 1) // MOE_ROWS * MOE_ROWS
        pad_end = jnp.cumsum(padded)
        pad_start = pad_end - padded
        start_row = jnp.zeros((1, LANES), jnp.int32).at[0, :N_EXPERTS].set(pad_start)
        block_first = jnp.arange(n_blocks, dtype=jnp.int32) * MOE_ROWS
        block_e = jnp.minimum(jnp.sum(pad_end[None, :] <= block_first[:, None], axis=1), N_EXPERTS - 1)
        n_used = (pad_end[-1] // MOE_ROWS).astype(jnp.int32).reshape(1)
        dest = _dest_call(e_pad, rank_pad, start_row)
        dst3 = dest[:, :TOP_K].reshape(t // ROW_TILE, 1, ROW_TILE * TOP_K)
        fill_lo = jnp.concatenate([pad_start + counts, pad_end[-1:]])
        fill_hi = jnp.concatenate([pad_end, jnp.full((1,), n_rows_max, jnp.int32)])
        xs = _dispatch_call(fill_lo, fill_hi, dst3, h2, n_rows_max)
        y = _moe_call(block_e.astype(jnp.int32), n_used, xs, w_up, b_up, w_down, b_down, l)
        xall = _combine_call(dst3, xall, mod_tab[l], gate, y, tiles_per_b)

    return _final_call(xall.reshape(bsz, s, d), final_g.reshape(1, d), n_ctx)
```

```python
import functools
import math

import jax
import jax.numpy as jnp
import numpy as np
from jax import lax
from jax.experimental import pallas as pl
from jax.experimental.pallas import tpu as pltpu

F32 = jnp.float32
BF16 = jnp.bfloat16
HIGHEST = lax.Precision.HIGHEST

D_MODEL = 1024
GRID_W = 64
EPS = 1e-6
ROPE_BASE = 10000.0
CONV_W = 5
MLA_HEADS = 8
MLA_NOPE = 64
MLA_ROPE = 32
MLA_V = 64
MLA_Q_LORA = 384
MLA_KV_LORA = 256
MLA_SCALE = (MLA_NOPE + MLA_ROPE) ** -0.5
LOG2_E = math.log2(math.e)
ML_HEADS = 4
ML_QK = 64
ML_V = 128
ML_QK_W = ML_HEADS * ML_QK
ML_V_W = ML_HEADS * ML_V
SSD_HEADS = 16
SSD_P = 64
SSD_N = 128
SSD_GROUPS = 4
SSD_HPG = SSD_HEADS // SSD_GROUPS
SSD_INNER = SSD_HEADS * SSD_P
SSD_BC_W = SSD_GROUPS * SSD_N
N_EXPERTS = 32
TOP_K = 4
D_FF = 1024
SWIGLU_LIMIT = 7.0
SWIGLU_ALPHA = 1.702

LANES = 128
SUBLANES = 8
HEAD_PAD = 128
ROW_SLABS = D_MODEL // LANES
V_ONES_LANE = (MLA_V, 0)
VMEM_LIMIT = 56 * 1024 * 1024

ROW_TILE = 256
CHUNK = 128
MOE_ROWS = 256
ATTN_HEADS = 4
ATTN_AHEAD = 2
DMA_ISSUE_UNROLL = 8

OFF_Q = 0
OFF_KV = OFF_Q + MLA_Q_LORA
OFF_SM = OFF_KV + MLA_KV_LORA
OFF_CV = OFF_SM + LANES
CV_W = 2 * ML_QK_W + SSD_INNER + 2 * SSD_BC_W
OFF_V = OFF_CV + CV_W
OFF_O = OFF_V + ML_V_W
OFF_Z = OFF_O + ML_V_W
OFF_G = OFF_Z + SSD_INNER
IN_PACKED = OFF_G + 3 * D_MODEL
SM_KR = 0
SM_KROT = MLA_ROPE
SM_IF = 2 * MLA_ROPE
SM_DT = SM_IF + 4 * ML_HEADS
CV_X = 0
CV_B = CV_X + SSD_INNER
CV_C = CV_B + SSD_BC_W
CV_Q = CV_C + SSD_BC_W
CV_K = CV_Q + ML_QK_W
CV_SSD_W = CV_Q


def _params(sem, vmem=VMEM_LIMIT):
    return pltpu.CompilerParams(dimension_semantics=sem, vmem_limit_bytes=vmem)


def _dot(a, b):
    return jnp.dot(a, b, preferred_element_type=F32)


def _dot_nt(a, b):
    return lax.dot_general(a, b, (((1,), (1,)), ((), ())), preferred_element_type=F32)


def _dot_hi(a, b):
    return jnp.dot(a, b, preferred_element_type=F32, precision=HIGHEST)


def _sigmoid(x):
    return 0.5 * jnp.tanh(0.5 * x) + 0.5


def _softplus(x):
    return jnp.maximum(x, 0.0) + jnp.log(1.0 + jnp.exp(-jnp.abs(x)))


def _log_sigmoid(x):
    return jnp.minimum(x, 0.0) - jnp.log(1.0 + jnp.exp(-jnp.abs(x)))


def _rms(x, g):
    return x * lax.rsqrt(jnp.mean(x * x, axis=-1, keepdims=True) + EPS) * g


assert ROW_SLABS == SUBLANES


def _to_tiles(ref, val):
    n = val.shape[0]
    for s in range(ROW_SLABS):
        ref[pl.ds(s, n, stride=ROW_SLABS), :] = val[:, s * LANES:(s + 1) * LANES]


def _from_tiles(ref, n):
    return jnp.concatenate([ref[pl.ds(s, n, stride=ROW_SLABS), :] for s in range(ROW_SLABS)], axis=1)


def _ada_kernel(c_ref, w_ref, b_ref, o_ref):
    c = c_ref[...]
    o_ref[0] = _dot_hi(c * _sigmoid(c), w_ref[0]) + b_ref[0]


def _ada_call(cond, w_ada, b_ada):
    depth, d, n = w_ada.shape
    tn = 1536
    return pl.pallas_call(
        _ada_kernel,
        grid=(depth, n // tn),
        in_specs=[pl.BlockSpec((cond.shape[0], d), lambda l, j: (0, 0)),
                  pl.BlockSpec((1, d, tn), lambda l, j: (l, 0, j)),
                  pl.BlockSpec((1, 1, tn), lambda l, j: (l, 0, j))],
        out_specs=pl.BlockSpec((1, cond.shape[0], tn), lambda l, j: (l, 0, j)),
        out_shape=jax.ShapeDtypeStruct((depth, cond.shape[0], n), F32),
        compiler_params=_params(("arbitrary", "arbitrary")),
        name="ada_mod",
    )(cond, w_ada, b_ada.reshape(depth, 1, n))


def _in_kernel(x_ref, mod_ref, g1_ref, w_ref, wst_ref, gq_ref, wqm_ref, wqr_ref, gkv_ref, wk_ref, wv_ref,
               vone_ref, e_ref, t1_ref, cs_ref, sn_ref,
               q_ref, k_ref, vv_ref, smc_ref, smt_ref, cv_ref, v_ref, o_ref, z_ref, g_ref):
    x = x_ref[...]
    mod = mod_ref[0, 0]
    hn = _rms(x, g1_ref[...]) * (1.0 + mod[1:2]) + mod[0:1]
    hb = hn.astype(BF16)

    def proj(lo, hi):
        return _dot(hb, w_ref[:, lo:hi])

    sm = proj(OFF_SM, OFF_CV)
    smc_ref[...] = sm
    smt_ref[...] = _dot_nt(wst_ref[...], hb)
    cv_ref[...] = proj(OFF_CV, OFF_V).astype(BF16)
    v_ref[...] = proj(OFF_V, OFF_O).astype(BF16)
    o_ref[...] = proj(OFF_O, OFF_Z).astype(BF16)
    z_ref[...] = proj(OFF_Z, OFF_G).astype(BF16)
    g_ref[...] = proj(OFF_G, IN_PACKED).astype(BF16)

    qn = _rms(proj(OFF_Q, OFF_KV), gq_ref[...]).astype(BF16)
    qm = _dot(qn, wqm_ref[...])
    qr = _dot(qn, wqr_ref[...])
    cs = cs_ref[...]
    sn = sn_ref[...]
    for h in range(MLA_HEADS):
        sl = slice(h * HEAD_PAD, (h + 1) * HEAD_PAD)
        q_ref[:, sl] = ((qm[:, sl] * cs + qr[:, sl] * sn) * (MLA_SCALE * LOG2_E)).astype(BF16)
    kvn = _rms(proj(OFF_KV, OFF_SM), gkv_ref[...]).astype(BF16)
    kro = (sm * t1_ref[...]).astype(BF16)
    k_ref[...] = (_dot(kvn, wk_ref[...]) + _dot(kro, e_ref[...])).astype(BF16)
    vv_ref[...] = (_dot(kvn, wv_ref[...]) + vone_ref[...]).astype(BF16)


def _in_call(x, mod, g1, wp, wst, gq, wqm, wqr, gkv, wk, wv, vone, emat, t1, cs, sn, tiles_per_b):
    t, d = x.shape
    n_tiles = t // ROW_TILE

    def const(shape):
        return pl.BlockSpec(shape, lambda i: (0,) * len(shape), pipeline_mode=pl.Buffered(1))

    def rows(w):
        return pl.BlockSpec((ROW_TILE, w), lambda i: (i, 0))

    def tab():
        return pl.BlockSpec((ROW_TILE, LANES), lambda i: (i % tiles_per_b, 0))

    widths = [MLA_HEADS * HEAD_PAD, MLA_HEADS * HEAD_PAD, MLA_HEADS * HEAD_PAD, LANES, None, CV_W, ML_V_W, ML_V_W,
              SSD_INNER, 3 * D_MODEL]
    dtypes = [BF16, BF16, BF16, F32, F32, BF16, BF16, BF16, BF16, BF16]
    out_specs, out_shape = [], []
    for w, dt in zip(widths, dtypes):
        if w is None:
            out_specs.append(pl.BlockSpec((LANES, ROW_TILE), lambda i: (0, i)))
            out_shape.append(jax.ShapeDtypeStruct((LANES, t), dt))
        else:
            out_specs.append(rows(w))
            out_shape.append(jax.ShapeDtypeStruct((t, w), dt))
    return pl.pallas_call(
        _in_kernel,
        grid=(n_tiles,),
        in_specs=[rows(d),
                  pl.BlockSpec((1, 1, 6, d), lambda i: (i // tiles_per_b, jnp.minimum(i % tiles_per_b, 1), 0, 0)),
                  const((1, d)), const(wp.shape), const(wst.shape), const((1, MLA_Q_LORA)), const(wqm.shape),
                  const(wqr.shape), const((1, MLA_KV_LORA)), const(wk.shape), const(wv.shape), const(vone.shape),
                  const(emat.shape), tab(), tab(), tab()],
        out_specs=out_specs,
        out_shape=out_shape,
        compiler_params=_params(("arbitrary",)),
        name="in_proj",
    )(x, mod, g1, wp, wst, gq, wqm, wqr, gkv, wk, wv, vone, emat, t1, cs, sn)


def _conv_kernel(x_ref, w_ref, b_ref, s_ref, o_ref, *, n_ctx):
    x = x_ref[0].astype(F32)
    s = x.shape[0]
    t = lax.broadcasted_iota(jnp.int32, x.shape, 0)
    is_ctx = t < n_ctx
    pos = jnp.where(is_ctx, t, t - n_ctx)
    length = jnp.where(is_ctx, n_ctx, s - n_ctx)
    acc = jnp.zeros_like(x) + b_ref[...]
    for j in range(CONV_W):
        d = j - CONV_W // 2
        xs = x if d == 0 else pltpu.roll(x, (-d) % s, 0)
        valid = (pos + d >= 0) & (pos + d < length)
        acc = acc + jnp.where(valid, xs, 0.0) * w_ref[j:j + 1, :]
    o_ref[0] = (acc * _sigmoid(acc) * s_ref[...]).astype(o_ref.dtype)


def _conv_call(cv, w, b, post, n_ctx):
    bsz, s, c = cv.shape
    return pl.pallas_call(
        functools.partial(_conv_kernel, n_ctx=n_ctx),
        grid=(bsz, c // LANES),
        in_specs=[pl.BlockSpec((1, s, LANES), lambda bi, j: (bi, 0, j)),
                  pl.BlockSpec((8, LANES), lambda bi, j: (0, j)),
                  pl.BlockSpec((1, LANES), lambda bi, j: (0, j)),
                  pl.BlockSpec((1, LANES), lambda bi, j: (0, j))],
        out_specs=pl.BlockSpec((1, s, LANES), lambda bi, j: (bi, 0, j)),
        out_shape=jax.ShapeDtypeStruct(cv.shape, BF16),
        compiler_params=_params(("arbitrary", "arbitrary")),
        name="dwconv_silu",
    )(cv, w, b, post)


def _attn_kernel(q_ref, k_ref, v_ref, o_ref, *, n_ctx):
    qi = pl.program_id(2)

    def attend(n_keys):
        heads = [slice(j * HEAD_PAD, (j + 1) * HEAD_PAD) for j in range(ATTN_HEADS)]

        def scores(j):
            return _dot_nt(q_ref[0, :, heads[j]], k_ref[0, 0:n_keys, heads[j]])

        pending = [scores(j) for j in range(ATTN_AHEAD)]
        outs = []
        for j in range(ATTN_HEADS):
            s = pending.pop(0)
            p = jnp.exp2(s - jnp.max(s, axis=-1, keepdims=True))
            if j + ATTN_AHEAD < ATTN_HEADS:
                pending.append(scores(j + ATTN_AHEAD))
            pv = _dot(p.astype(BF16), v_ref[0, 0:n_keys, heads[j]])
            ones = V_ONES_LANE[j % 2]
            outs.append(pv / pv[:, ones:ones + 1])
        lane = lax.broadcasted_iota(jnp.int32, outs[0].shape, 1)
        for pp in range(ATTN_HEADS // 2):
            o_ref[0, :, pp * LANES:(pp + 1) * LANES] = jnp.where(lane < MLA_V, outs[2 * pp],
                                                                 outs[2 * pp + 1]).astype(o_ref.dtype)

    @pl.when(qi == 0)
    def _():
        attend(n_ctx)

    @pl.when(qi != 0)
    def _():
        attend(k_ref.shape[1])


def _attn_call(q, k, v, n_ctx):
    bsz, s, _ = q.shape
    assert n_ctx == ROW_TILE
    w_in = ATTN_HEADS * HEAD_PAD
    return pl.pallas_call(
        functools.partial(_attn_kernel, n_ctx=n_ctx),
        grid=(bsz, MLA_HEADS // ATTN_HEADS, s // ROW_TILE),
        in_specs=[pl.BlockSpec((1, ROW_TILE, w_in), lambda b, h, i: (b, i, h)),
                  pl.BlockSpec((1, s, w_in), lambda b, h, i: (b, 0, h)),
                  pl.BlockSpec((1, s, w_in), lambda b, h, i: (b, 0, h))],
        out_specs=pl.BlockSpec((1, ROW_TILE, ATTN_HEADS * MLA_V), lambda b, h, i: (b, i, h)),
        out_shape=jax.ShapeDtypeStruct((bsz, s, MLA_HEADS * MLA_V), BF16),
        compiler_params=_params(("arbitrary", "arbitrary", "arbitrary")),
        name="mla_attention",
    )(q, k, v)


def _tri(n, upper):
    r = lax.broadcasted_iota(jnp.int32, (n, n), 0)
    c = lax.broadcasted_iota(jnp.int32, (n, n), 1)
    return (r <= c) if upper else (r >= c)


def _bwd_chunk(j, nc_ctx, nc):
    return jnp.where(j < nc_ctx, nc_ctx - 1 - j, nc + nc_ctx - 1 - j)


def _mlstm_body(qkf_ref, vf_ref, gcf_ref, grf_ref, qkb_ref, vb_ref, gcb_ref, grb_ref, brow_ref, bcol_ref,
                hf_ref, hb_ref, st_ref, m_ref):
    L = CHUNK
    lane = lax.broadcasted_iota(jnp.int32, (L, LANES), 1)
    ones_col = jnp.where(lane == 0, 1.0, 0.0).astype(F32)
    lane_qk = lax.broadcasted_iota(jnp.int32, (L, ML_QK_W), 1)

    dirs = []
    for d, (qk_ref, v_ref, gc_ref, gr_ref, out_ref) in enumerate(
            [(qkf_ref, vf_ref, gcf_ref, grf_ref, hf_ref), (qkb_ref, vb_ref, gcb_ref, grb_ref, hb_ref)]):
        rev = d == 1
        gc = gc_ref[...] + brow_ref[...]
        gr = gr_ref[...] + bcol_ref[...]
        k_all = qk_ref[0, :, ML_QK_W:2 * ML_QK_W]
        dirs.append(dict(
            gc=gc, gr=gr, v_ref=v_ref, out_ref=out_ref, k_all=k_all, q_all=qk_ref[0, :, 0:ML_QK_W],
            fc_all=_dot_hi(jnp.where(_tri(L, rev), 1.0, 0.0).astype(F32), _log_sigmoid(gc)),
            fr_all=_dot_hi(_log_sigmoid(gr), jnp.where(_tri(L, not rev), 1.0, 0.0).astype(F32)),
            mask=_tri(L, rev),
            end=0 if rev else L - 1,
            kt_all=jnp.transpose(k_all.astype(F32)).astype(BF16),
            state=st_ref[d]))
    probs = [(d, h) for d in range(2) for h in range(ML_HEADS)]

    def stage_gates(d, h):
        dd = dirs[d]
        li_lane = SM_IF + d * 2 * ML_HEADS + h
        lf_lane = li_lane + ML_HEADS
        fc = dd['fc_all'][:, lf_lane:lf_lane + 1]
        fr = dd['fr_all'][lf_lane:lf_lane + 1, :]
        ic = dd['gc'][:, li_lane:li_lane + 1]
        ir = dd['gr'][li_lane:li_lane + 1, :]
        last = fc[dd['end']:dd['end'] + 1, :]
        m_old = m_ref[d * ML_HEADS + h][0:1, 0:1]
        m_new = jnp.maximum(last + m_old, jnp.max(last - fr + ir, axis=-1, keepdims=True))
        dmat = jnp.where(dd['mask'], fc - fr + ir, -jnp.inf)
        g = fc + m_old
        return dict(m_new=m_new, w_src=jnp.exp(last - fc + ic - m_new), w_old=jnp.exp(last + m_old - m_new),
                    dmat=dmat, g=g, m_row=jnp.maximum(g, jnp.max(dmat, axis=-1, keepdims=True)))

    gates = [stage_gates(d, h) for d, h in probs]
    q_hs = [jnp.where((lane_qk >= h * ML_QK) & (lane_qk < (h + 1) * ML_QK), dirs[d]['q_all'], 0).astype(BF16)
            for d, h in probs]
    qk = [_dot_nt(q_hs[i], dirs[d]['k_all']) for i, (d, h) in enumerate(probs)]
    qc = [_dot(q_hs[i], dirs[d]['state'].astype(BF16)) for i, (d, h) in enumerate(probs)]
    s = [qk[i] * jnp.exp(gates[i]['dmat'] - gates[i]['m_row']) for i in range(len(probs))]
    v_ext = [jnp.concatenate([dirs[d]['v_ref'][0, :, h * ML_V:(h + 1) * ML_V].astype(F32), ones_col], axis=1)
             for d, h in probs]
    tot = [_dot(s[i].astype(BF16), v_ext[i].astype(BF16)) + jnp.exp(gates[i]['g'] - gates[i]['m_row']) * qc[i]
           for i in range(len(probs))]
    upd = [_dot(dirs[d]['kt_all'][h * ML_QK:(h + 1) * ML_QK, :], (gates[i]['w_src'] * v_ext[i]).astype(BF16))
           for i, (d, h) in enumerate(probs)]
    for i, (d, h) in enumerate(probs):
        den = tot[i][:, ML_V:ML_V + 1]
        dirs[d]['out_ref'][0, :, h * ML_V:(h + 1) * ML_V] = (
            tot[i][:, :ML_V] / jnp.maximum(jnp.abs(den), jnp.exp(-gates[i]['m_row']))).astype(F32)
        rows = slice(h * ML_QK, (h + 1) * ML_QK)
        st_ref[d, rows, :] = gates[i]['w_old'] * dirs[d]['state'][rows, :] + upd[i]
        m_ref[d * ML_HEADS + h] = jnp.broadcast_to(gates[i]['m_new'], m_ref.shape[1:])


def _ssd_body(cvf_ref, gcf_ref, grf_ref, cvb_ref, gcb_ref, grb_ref, brow_ref, bcol_ref, arow_ref, acol_ref,
              yf_ref, yb_ref, st_ref):
    L = CHUNK
    gw = SSD_HPG * SSD_P
    lane_g = lax.broadcasted_iota(jnp.int32, (L, gw), 1)

    def per_head(cols):
        out = cols[SSD_HPG - 1]
        for i in range(SSD_HPG - 2, -1, -1):
            out = jnp.where(lane_g[0:cols[0].shape[0]] < (i + 1) * SSD_P, cols[i], out)
        return out

    dirs = []
    for d, (cv_ref, gc_ref, gr_ref, out_ref) in enumerate(
            [(cvf_ref, gcf_ref, grf_ref, yf_ref), (cvb_ref, gcb_ref, grb_ref, yb_ref)]):
        rev = d == 1
        dtc = _softplus(gc_ref[...] + brow_ref[...])
        dtr = _softplus(gr_ref[...] + bcol_ref[...])
        dirs.append(dict(
            cv_ref=cv_ref, out_ref=out_ref, dtc=dtc, dtr=dtr, mask=_tri(L, rev), end=0 if rev else L - 1,
            ac_all=_dot_hi(jnp.where(_tri(L, rev), 1.0, 0.0).astype(F32), dtc * arow_ref[...]),
            ar_all=_dot_hi(dtr * acol_ref[...], jnp.where(_tri(L, not rev), 1.0, 0.0).astype(F32))))
    groups = [(d, gi) for d in range(2) for gi in range(SSD_GROUPS)]

    def stage_group(d, gi):
        cv_ref = dirs[d]['cv_ref']
        x_g = cv_ref[0, :, CV_X + gi * gw:CV_X + (gi + 1) * gw]
        b_g = cv_ref[0, :, CV_B + gi * SSD_N:CV_B + (gi + 1) * SSD_N]
        c_g = cv_ref[0, :, CV_C + gi * SSD_N:CV_C + (gi + 1) * SSD_N]
        state = st_ref[d, gi]
        return dict(x_g=x_g, state=state, cb=_dot_nt(c_g, b_g), ch=_dot(c_g, state.astype(BF16)),
                    bt=jnp.transpose(b_g.astype(F32)).astype(BF16))

    def stage_head(d, gi, hg, grp):
        dd = dirs[d]
        ln = SM_DT + d * SSD_HEADS + gi * SSD_HPG + hg
        ac = dd['ac_all'][:, ln:ln + 1]
        ar = dd['ar_all'][ln:ln + 1, :]
        last = ac[dd['end']:dd['end'] + 1, :]
        seg = jnp.exp(jnp.where(dd['mask'], ac - ar, -jnp.inf))
        return dict(mm=(seg * grp['cb'] * dd['dtr'][ln:ln + 1, :]).astype(BF16), e=jnp.exp(ac),
                    w=jnp.exp(last - ac) * dd['dtc'][:, ln:ln + 1], last=jnp.exp(last))

    grps = [stage_group(d, gi) for d, gi in groups]
    heads = [[stage_head(d, gi, hg, grps[i]) for hg in range(SSD_HPG)] for i, (d, gi) in enumerate(groups)]
    ys = [[_dot(hd['mm'], grps[i]['x_g']) for hd in heads[i]] for i in range(len(groups))]
    for i, (d, gi) in enumerate(groups):
        grp = grps[i]
        y = ys[i][SSD_HPG - 1]
        for hg in range(SSD_HPG - 2, -1, -1):
            y = jnp.where(lane_g < (hg + 1) * SSD_P, ys[i][hg], y)
        y = y + grp['ch'] * per_head([hd['e'] for hd in heads[i]])
        dirs[d]['out_ref'][0, :, gi * gw:(gi + 1) * gw] = y
        xw = (grp['x_g'].astype(F32) * per_head([hd['w'] for hd in heads[i]])).astype(BF16)
        st_ref[d, gi] = per_head([hd['last'] for hd in heads[i]]) * grp['state'] + _dot(grp['bt'], xw)


def _mlstm_kernel(*refs):
    @pl.when(pl.program_id(1) == 0)
    def _():
        refs[-2][...] = jnp.zeros_like(refs[-2])
        refs[-1][...] = jnp.zeros_like(refs[-1])

    _mlstm_body(*refs)


def _ssd_kernel(*refs):
    @pl.when(pl.program_id(1) == 0)
    def _():
        refs[-1][...] = jnp.zeros_like(refs[-1])

    _ssd_body(*refs)


def _scan_call(cvo, v, smc, smt, brow, bcol, arow, acol, n_ctx):
    bsz, s, _ = cvo.shape
    nc, nc_ctx = s // CHUNK, n_ctx // CHUNK
    qk_w = 2 * ML_QK_W

    def fwd(b, j):
        return j

    def bwd(b, j):
        return _bwd_chunk(j, nc_ctx, nc)

    def gate_specs(ch):
        return [pl.BlockSpec((CHUNK, LANES), lambda b, j: (b * nc + ch(b, j), 0)),
                pl.BlockSpec((LANES, CHUNK), lambda b, j: (0, b * nc + ch(b, j)))]

    def ml_specs(ch):
        return [pl.BlockSpec((1, CHUNK, qk_w), lambda b, j: (b, ch(b, j), CV_Q // qk_w)),
                pl.BlockSpec((1, CHUNK, ML_V_W), lambda b, j: (b, ch(b, j), 0))] + gate_specs(ch)

    def ssd_specs(ch):
        return [pl.BlockSpec((1, CHUNK, CV_SSD_W), lambda b, j: (b, ch(b, j), 0))] + gate_specs(ch)

    def const(shape):
        return pl.BlockSpec(shape, lambda b, j: (0, 0))

    def out_spec(ch, w):
        return pl.BlockSpec((1, CHUNK, w), lambda b, j: (b, ch(b, j), 0))

    h_out = jax.ShapeDtypeStruct((bsz, s, ML_V_W), F32)
    y_out = jax.ShapeDtypeStruct((bsz, s, SSD_INNER), F32)
    hf, hb = pl.pallas_call(
        _mlstm_kernel,
        grid=(bsz, nc),
        in_specs=ml_specs(fwd) + ml_specs(bwd) + [const((1, LANES)), const((LANES, 1))],
        out_specs=[out_spec(fwd, ML_V_W), out_spec(bwd, ML_V_W)],
        out_shape=[h_out, h_out],
        scratch_shapes=[pltpu.VMEM((2, ML_QK_W, ML_V + LANES), F32),
                        pltpu.VMEM((2 * ML_HEADS, 8, LANES), F32)],
        compiler_params=_params(("arbitrary", "arbitrary")),
        name="mlstm_scan",
    )(cvo, v, smc, smt, cvo, v, smc, smt, brow, bcol)
    yf, yb = pl.pallas_call(
        _ssd_kernel,
        grid=(bsz, nc),
        in_specs=ssd_specs(fwd) + ssd_specs(bwd) + [const((1, LANES)), const((LANES, 1)), const((1, LANES)),
                                                    const((LANES, 1))],
        out_specs=[out_spec(fwd, SSD_INNER), out_spec(bwd, SSD_INNER)],
        out_shape=[y_out, y_out],
        scratch_shapes=[pltpu.VMEM((2, SSD_GROUPS, SSD_N, SSD_HPG * SSD_P), F32)],
        compiler_params=_params(("arbitrary", "arbitrary")),
        name="ssd_scan",
    )(cvo, smc, smt, cvo, smc, smt, brow, bcol, arow, acol)
    return hf, hb, yf, yb


def _out_kernel(x_ref, mod_ref, att_ref, hf_ref, hb_ref, yf_ref, yb_ref, sx_ref, z_ref, o_ref, g_ref,
                mlg_ref, sd_ref, sg_ref, wa_ref, wm_ref, ws_ref, wo_ref, g2_ref, wr_ref, br_ref,
                xo_ref, h2_ref, lg_ref):
    mod = mod_ref[0, 0]
    hm = hf_ref[...] + hb_ref[...]
    m_parts = []
    for h in range(ML_HEADS):
        sl = slice(h * ML_V, (h + 1) * ML_V)
        m_parts.append(_rms(hm[:, sl], mlg_ref[:, sl]))
    m_out = jnp.concatenate(m_parts, axis=1) * _sigmoid(o_ref[...].astype(F32))
    z = z_ref[...].astype(F32)
    y = (yf_ref[...] + yb_ref[...] + sd_ref[...] * sx_ref[...].astype(F32)) * (z * _sigmoid(z))
    gw = SSD_HPG * SSD_P
    s_parts = []
    for gi in range(SSD_GROUPS):
        sl = slice(gi * gw, (gi + 1) * gw)
        s_parts.append(_rms(y[:, sl], sg_ref[:, sl]))
    s_out = jnp.concatenate(s_parts, axis=1)
    gt = g_ref[...].astype(F32)
    d = D_MODEL
    merged = (_sigmoid(gt[:, 0:d]) * _dot(att_ref[...], wa_ref[...])
              + _sigmoid(gt[:, d:2 * d]) * _dot(m_out.astype(BF16), wm_ref[...])
              + _sigmoid(gt[:, 2 * d:3 * d]) * _dot(s_out.astype(BF16), ws_ref[...]))
    xn = x_ref[...] + mod[2:3] * _dot(merged.astype(BF16), wo_ref[...])
    xo_ref[...] = xn
    h2 = _rms(xn, g2_ref[...]) * (1.0 + mod[4:5]) + mod[3:4]
    _to_tiles(h2_ref, h2)
    lg_ref[...] = _dot_hi(h2, wr_ref[...]) + br_ref[...]


def _out_call(x, mod, att, hf, hb, yf, yb, cvo, z, o, g, mlg, sd, sg, wa, wm, ws, wo, g2, wr, br, tiles_per_b):
    t, d = x.shape
    n_tiles = t // ROW_TILE

    def const(shape):
        return pl.BlockSpec(shape, lambda i: (0,) * len(shape), pipeline_mode=pl.Buffered(1))

    def rows(w, blk=0):
        return pl.BlockSpec((ROW_TILE, w), lambda i: (i, blk))

    return pl.pallas_call(
        _out_kernel,
        grid=(n_tiles,),
        in_specs=[rows(d),
                  pl.BlockSpec((1, 1, 6, d), lambda i: (i // tiles_per_b, jnp.minimum(i % tiles_per_b, 1), 0, 0)),
                  rows(MLA_HEADS * MLA_V), rows(ML_V_W), rows(ML_V_W), rows(SSD_INNER), rows(SSD_INNER),
                  rows(SSD_INNER, CV_X // SSD_INNER), rows(SSD_INNER), rows(ML_V_W), rows(3 * d),
                  const((1, ML_V_W)), const((1, SSD_INNER)), const((1, SSD_INNER)),
                  const(wa.shape), const(wm.shape), const(ws.shape), const(wo.shape), const((1, d)),
                  const(wr.shape), const((1, LANES))],
        out_specs=[rows(d), pl.BlockSpec((ROW_TILE * ROW_SLABS, LANES), lambda i: (i, 0)), rows(LANES)],
        out_shape=[jax.ShapeDtypeStruct((t, d), F32), jax.ShapeDtypeStruct((t * ROW_SLABS, LANES), F32),
                   jax.ShapeDtypeStruct((t, LANES), F32)],
        compiler_params=_params(("arbitrary",)),
        name="mix_out",
    )(x, mod, att, hf, hb, yf, yb, cvo, z, o, g, mlg, sd, sg, wa, wm, ws, wo, g2, wr, br)


def _route_kernel(lg_ref, e_ref, gate_ref, rank_ref, cnt_ref, base_ref):
    i = pl.program_id(0)

    @pl.when(i == 0)
    def _():
        base_ref[...] = jnp.zeros_like(base_ref)

    v = lg_ref[...]
    n = v.shape[0]
    lane = lax.broadcasted_iota(jnp.int32, v.shape, 1)
    v = jnp.where(lane < N_EXPERTS, v, -jnp.inf)
    tops, hots = [], []
    e_out = jnp.zeros(v.shape, jnp.int32)
    for k in range(TOP_K):
        mk = jnp.max(v, axis=-1, keepdims=True)
        idx = jnp.min(jnp.where(v == mk, lane, LANES), axis=-1, keepdims=True)
        hot = lane == idx
        v = jnp.where(hot, -jnp.inf, v)
        tops.append(mk)
        hots.append(hot)
        e_out = jnp.where(lane == k, idx, e_out)
    ex = [jnp.exp(tk - tops[0]) for tk in tops]
    den = ex[0]
    for k in range(1, TOP_K):
        den = den + ex[k]
    gate = jnp.zeros(v.shape, F32)
    for k in range(TOP_K):
        gate = jnp.where(lane == k, ex[k] / den, gate)
    hot_sum = jnp.zeros(v.shape, F32)
    for k in range(TOP_K):
        hot_sum = hot_sum + jnp.where(hots[k], 1.0, 0.0)
    r = lax.broadcasted_iota(jnp.int32, (n, n), 0)
    c = lax.broadcasted_iota(jnp.int32, (n, n), 1)
    before = jnp.where(c < r, 1.0, 0.0).astype(BF16)
    base = base_ref[0:1, :]
    cum = _dot(before, hot_sum.astype(BF16)) + base
    rank = jnp.zeros(v.shape, jnp.int32)
    for k in range(TOP_K):
        rk = jnp.sum(jnp.where(hots[k], cum, 0.0), axis=-1, keepdims=True)
        rank = jnp.where(lane == k, rk.astype(jnp.int32), rank)
    e_ref[...] = e_out
    gate_ref[...] = gate
    rank_ref[...] = rank
    new_base = base + jnp.sum(hot_sum, axis=0, keepdims=True)
    base_ref[...] = jnp.broadcast_to(new_base, base_ref.shape)
    cnt_ref[...] = jnp.broadcast_to(new_base, cnt_ref.shape)


def _route_call(logits):
    t = logits.shape[0]

    def rows():
        return pl.BlockSpec((ROW_TILE, LANES), lambda i: (i, 0))

    return pl.pallas_call(
        _route_kernel,
        grid=(t // ROW_TILE,),
        in_specs=[rows()],
        out_specs=[rows(), rows(), rows(), pl.BlockSpec((8, LANES), lambda i: (0, 0))],
        out_shape=[jax.ShapeDtypeStruct((t, LANES), jnp.int32), jax.ShapeDtypeStruct((t, LANES), F32),
                   jax.ShapeDtypeStruct((t, LANES), jnp.int32), jax.ShapeDtypeStruct((8, LANES), F32)],
        scratch_shapes=[pltpu.VMEM((8, LANES), F32)],
        compiler_params=_params(("arbitrary",)),
        name="moe_route",
    )(logits)


def _dest_kernel(e_ref, rank_ref, start_ref, d_ref):
    e = e_ref[...]
    rank = rank_ref[...]
    lane = lax.broadcasted_iota(jnp.int32, e.shape, 1)
    out = jnp.zeros(e.shape, jnp.int32)
    for k in range(TOP_K):
        base = jnp.sum(jnp.where(lane == e[:, k:k + 1], start_ref[...], 0), axis=-1, keepdims=True)
        out = jnp.where(lane == k, (base + rank[:, k:k + 1]) * SUBLANES, out)
    d_ref[...] = out


def _dest_call(e_pad, rank_pad, start_row):
    t = e_pad.shape[0]

    def rows():
        return pl.BlockSpec((ROW_TILE, LANES), lambda i: (i, 0))

    return pl.pallas_call(
        _dest_kernel,
        grid=(t // ROW_TILE,),
        in_specs=[rows(), rows(), pl.BlockSpec((1, LANES), lambda i: (0, 0))],
        out_specs=rows(),
        out_shape=jax.ShapeDtypeStruct((t, LANES), jnp.int32),
        compiler_params=_params(("arbitrary",)),
        name="moe_dest",
    )(e_pad, rank_pad, start_row)


def _dispatch_kernel(lo_ref, hi_ref, dst_ref, h_ref, xs_hbm, stage, zrow, sem, zsem):
    i = pl.program_id(0)
    n = pl.num_programs(0)
    slot = i % 2
    stage[slot] = h_ref[...]

    def tile(ref, first):
        return ref.at[pl.ds(pl.multiple_of(first, SUBLANES), SUBLANES), :]

    def zero_copy(r):
        return pltpu.make_async_copy(zrow, tile(xs_hbm, r * SUBLANES), zsem)

    @pl.when(i == 0)
    def _():
        zrow[...] = jnp.zeros_like(zrow)
        for e in range(N_EXPERTS + 1):
            def start(r, carry):
                zero_copy(r).start()
                return carry
            lax.fori_loop(lo_ref[e], hi_ref[e], start, 0)
        for e in range(N_EXPERTS + 1):
            def wait(r, carry):
                zero_copy(r).wait()
                return carry
            lax.fori_loop(lo_ref[e], hi_ref[e], wait, 0)

    def body(r, carry):
        src = tile(stage.at[slot], r * SUBLANES)
        for k in range(TOP_K):
            pltpu.make_async_copy(src, tile(xs_hbm, dst_ref[0, 0, r * TOP_K + k]), sem.at[slot]).start()
        return carry
    lax.fori_loop(0, ROW_TILE, body, 0, unroll=DMA_ISSUE_UNROLL)

    def wait_step(s):
        for _ in range(TOP_K):
            pltpu.make_async_copy(stage.at[s], xs_hbm.at[pl.ds(0, ROW_TILE * SUBLANES), :], sem.at[s]).wait()

    @pl.when(i > 0)
    def _():
        wait_step(1 - slot)

    @pl.when(i == n - 1)
    def _():
        wait_step(slot)


def _dispatch_call(fill_lo, fill_hi, dst3, h2, n_rows):
    n_tiles = dst3.shape[0]
    grid_spec = pltpu.PrefetchScalarGridSpec(
        num_scalar_prefetch=2,
        grid=(n_tiles,),
        in_specs=[pl.BlockSpec((1, 1, ROW_TILE * TOP_K), lambda i, lo, hi: (i, 0, 0), memory_space=pltpu.SMEM),
                  pl.BlockSpec((ROW_TILE * ROW_SLABS, LANES), lambda i, lo, hi: (i, 0))],
        out_specs=pl.BlockSpec(memory_space=pl.ANY),
        scratch_shapes=[pltpu.VMEM((2, ROW_TILE * ROW_SLABS, LANES), F32), pltpu.VMEM((SUBLANES, LANES), F32),
                        pltpu.SemaphoreType.DMA((2,)), pltpu.SemaphoreType.DMA(())],
    )
    return pl.pallas_call(
        _dispatch_kernel,
        grid_spec=grid_spec,
        out_shape=jax.ShapeDtypeStruct((n_rows * ROW_SLABS, LANES), F32),
        compiler_params=_params(("arbitrary",)),
        name="moe_dispatch",
    )(fill_lo, fill_hi, dst3, h2)


def _moe_kernel(be_ref, nb_ref, x_ref, wu_ref, bu_ref, wd_ref, bd_ref, y_ref, wub, wdb):
    i = pl.program_id(0)

    @pl.when(i < nb_ref[0])
    def _():
        prev = be_ref[jnp.maximum(i - 1, 0)]

        @pl.when((i == 0) | (be_ref[i] != prev))
        def _():
            wub[...] = wu_ref[...].astype(BF16)
            wdb[...] = wd_ref[...].astype(BF16)

        gu = _dot(_from_tiles(x_ref, MOE_ROWS).astype(BF16), wub[...]) + bu_ref[...]
        glu = jnp.minimum(gu[:, :D_FF], SWIGLU_LIMIT)
        lin = jnp.clip(gu[:, D_FF:], -SWIGLU_LIMIT, SWIGLU_LIMIT)
        act = glu * _sigmoid(SWIGLU_ALPHA * glu) * (lin + 1.0)
        _to_tiles(y_ref, _dot(act.astype(BF16), wdb[...]) + bd_ref[...])

    @pl.when(i >= nb_ref[0])
    def _():
        y_ref[...] = jnp.zeros_like(y_ref)


def _moe_call(block_e, n_used, xs, w_up, b_up, w_down, b_down, layer):
    n_blocks = block_e.shape[0]
    d = D_MODEL
    blk_rows = MOE_ROWS * ROW_SLABS
    b_up = b_up.reshape(b_up.shape[0], N_EXPERTS, 1, 2 * D_FF)
    b_down = b_down.reshape(b_down.shape[0], N_EXPERTS, 1, d)

    def blk(i, nb):
        return jnp.minimum(i, nb[0] - 1)

    def wspec(shape):
        return pl.BlockSpec((None, None) + shape, lambda i, be, nb: (layer, be[blk(i, nb)], 0, 0))

    grid_spec = pltpu.PrefetchScalarGridSpec(
        num_scalar_prefetch=2,
        grid=(n_blocks,),
        in_specs=[pl.BlockSpec((blk_rows, LANES), lambda i, be, nb: (blk(i, nb), 0)),
                  wspec((d, 2 * D_FF)), wspec((1, 2 * D_FF)), wspec((D_FF, d)), wspec((1, d))],
        out_specs=pl.BlockSpec((blk_rows, LANES), lambda i, be, nb: (i, 0)),
        scratch_shapes=[pltpu.VMEM((d, 2 * D_FF), BF16), pltpu.VMEM((D_FF, d), BF16)],
    )
    return pl.pallas_call(
        _moe_kernel,
        grid_spec=grid_spec,
        out_shape=jax.ShapeDtypeStruct((n_blocks * blk_rows, LANES), F32),
        compiler_params=_params(("arbitrary",)),
        name="moe_experts",
    )(block_e, n_used, xs, w_up, b_up, w_down, b_down)


def _combine_kernel(dst_ref, dstn_ref, x_ref, mod_ref, gate_ref, y_hbm, o_ref, ybuf, sem):
    i = pl.program_id(0)
    n = pl.num_programs(0)
    slot = i % 2

    def gather(dref, dst_slot):
        def body(r, carry):
            for k in range(TOP_K):
                first = pl.multiple_of(dref[0, 0, r * TOP_K + k], SUBLANES)
                pltpu.make_async_copy(y_hbm.at[pl.ds(first, SUBLANES), :],
                                      ybuf.at[dst_slot, k, pl.ds(pl.multiple_of(r * SUBLANES, SUBLANES), SUBLANES), :],
                                      sem.at[dst_slot]).start()
            return carry
        lax.fori_loop(0, ROW_TILE, body, 0, unroll=DMA_ISSUE_UNROLL)

    @pl.when(i == 0)
    def _():
        gather(dst_ref, 0)

    @pl.when(i + 1 < n)
    def _():
        gather(dstn_ref, 1 - slot)

    for k in range(TOP_K):
        pltpu.make_async_copy(y_hbm.at[pl.ds(0, ROW_TILE * SUBLANES), :], ybuf.at[slot, k], sem.at[slot]).wait()
    gate = gate_ref[...]
    f = gate[:, 0:1] * _from_tiles(ybuf.at[slot, 0], ROW_TILE)
    for k in range(1, TOP_K):
        f = f + gate[:, k:k + 1] * _from_tiles(ybuf.at[slot, k], ROW_TILE)
    o_ref[...] = x_ref[...] + mod_ref[0, 0][5:6] * f


def _combine_call(dst3, x, mod, gate, y, tiles_per_b):
    t, d = x.shape
    n_tiles = t // ROW_TILE
    return pl.pallas_call(
        _combine_kernel,
        grid=(n_tiles,),
        in_specs=[pl.BlockSpec((1, 1, ROW_TILE * TOP_K), lambda i: (i, 0, 0), memory_space=pltpu.SMEM),
                  pl.BlockSpec((1, 1, ROW_TILE * TOP_K), lambda i: (jnp.minimum(i + 1, n_tiles - 1), 0, 0),
                               memory_space=pltpu.SMEM),
                  pl.BlockSpec((ROW_TILE, d), lambda i: (i, 0)),
                  pl.BlockSpec((1, 1, 6, d), lambda i: (i // tiles_per_b, jnp.minimum(i % tiles_per_b, 1), 0, 0)),
                  pl.BlockSpec((ROW_TILE, LANES), lambda i: (i, 0)),
                  pl.BlockSpec(memory_space=pl.ANY)],
        out_specs=pl.BlockSpec((ROW_TILE, d), lambda i: (i, 0)),
        out_shape=jax.ShapeDtypeStruct((t, d), F32),
        scratch_shapes=[pltpu.VMEM((2, TOP_K, ROW_TILE * ROW_SLABS, LANES), F32),
                        pltpu.SemaphoreType.DMA((2,))],
        compiler_params=_params(("arbitrary",)),
        name="moe_combine",
    )(dst3, dst3, x, mod, gate, y)


def _final_kernel(x_ref, g_ref, o_ref):
    o_ref[0] = _rms(x_ref[0], g_ref[...])


def _final_call(x3, g, n_ctx):
    bsz, s, d = x3.shape
    skip = n_ctx // ROW_TILE
    return pl.pallas_call(
        _final_kernel,
        grid=(bsz, (s - n_ctx) // ROW_TILE),
        in_specs=[pl.BlockSpec((1, ROW_TILE, d), lambda b, i: (b, i + skip, 0)),
                  pl.BlockSpec((1, d), lambda b, i: (0, 0))],
        out_specs=pl.BlockSpec((1, ROW_TILE, d), lambda b, i: (b, i, 0)),
        out_shape=jax.ShapeDtypeStruct((bsz, s - n_ctx, d), F32),
        compiler_params=_params(("arbitrary", "arbitrary")),
        name="final_norm",
    )(x3, g)


def _rot_cols(w):
    half = MLA_ROPE // 2
    return jnp.concatenate([-w[..., half:], w[..., :half]], axis=-1)


def _pack_weights(w_in, mla_w_uq, mla_w_ukv):
    depth = w_in.shape[0]
    sizes = (MLA_Q_LORA, MLA_KV_LORA, MLA_ROPE, 2 * ML_QK_W, ML_V_W, ML_V_W, 4 * ML_HEADS, SSD_INNER,
             SSD_INNER + 2 * SSD_BC_W, 2 * SSD_HEADS, 3 * D_MODEL)
    idx = [int(v) for v in np.cumsum(sizes)[:-1]]
    w_q, w_kv, w_kr, w_qk, w_v, w_o, w_if, w_z, w_xbc, w_dt, w_g = jnp.split(w_in, idx, axis=-1)
    pad = jnp.zeros(w_in.shape[:2] + (LANES - SM_DT - 2 * SSD_HEADS,), w_in.dtype)
    small = jnp.concatenate([w_kr, _rot_cols(w_kr), w_if, w_dt, pad], axis=-1)
    wp = jnp.concatenate([w_q, w_kv, small, w_xbc, w_qk, w_v, w_o, w_z, w_g], axis=-1).astype(BF16)
    wst = jnp.swapaxes(small, 1, 2).astype(BF16)

    wq = mla_w_uq.reshape(depth, MLA_Q_LORA, MLA_HEADS, MLA_NOPE + MLA_ROPE)
    nope, rope = wq[..., :MLA_NOPE], wq[..., MLA_NOPE:]
    zr = jnp.zeros_like(rope)
    zn = jnp.zeros_like(nope)
    wqm = jnp.concatenate([nope, rope, zr], axis=-1).reshape(depth, MLA_Q_LORA, -1).astype(BF16)
    wqr = jnp.concatenate([zn, _rot_cols(rope), zr], axis=-1).reshape(depth, MLA_Q_LORA, -1).astype(BF16)
    wkv = mla_w_ukv.reshape(depth, MLA_KV_LORA, MLA_HEADS, MLA_NOPE + MLA_V)
    knope, val = wkv[..., :MLA_NOPE], wkv[..., MLA_NOPE:]
    wk = jnp.concatenate([knope, jnp.zeros_like(knope)], axis=-1).reshape(depth, MLA_KV_LORA, -1).astype(BF16)
    zv = jnp.zeros_like(val)
    val_even = jnp.concatenate([val, zv], axis=-1)[:, :, 0::2]
    val_odd = jnp.concatenate([zv, val], axis=-1)[:, :, 1::2]
    wv = jnp.stack([val_even, val_odd], axis=3).reshape(depth, MLA_KV_LORA, -1).astype(BF16)
    return wp, wst, wqm, wqr, wk, wv


def _value_ones_row():
    e = np.zeros((1, MLA_HEADS * HEAD_PAD), np.float32)
    for h in range(MLA_HEADS):
        e[0, h * HEAD_PAD + V_ONES_LANE[h % 2]] = 1.0
    return jnp.asarray(e)


def _place_matrix():
    e = np.zeros((LANES, MLA_HEADS * HEAD_PAD), np.float32)
    for h in range(MLA_HEADS):
        for jj in range(MLA_ROPE):
            e[SM_KR + jj, h * HEAD_PAD + MLA_NOPE + jj] = 1.0
            e[SM_KROT + jj, h * HEAD_PAD + MLA_NOPE + jj] = 1.0
    return jnp.asarray(e, BF16)


def _rope_tables(n_ctx, n_lat):
    rows = n_lat // GRID_W
    row = jnp.broadcast_to(jnp.arange(rows)[:, None], (rows, GRID_W)).reshape(-1)
    col = jnp.broadcast_to(jnp.arange(GRID_W)[None, :], (rows, GRID_W)).reshape(-1)
    n_freq = MLA_ROPE // 4
    inv = ROPE_BASE ** (-jnp.arange(n_freq, dtype=F32) / n_freq)
    ang = jnp.concatenate([row[:, None] * inv, col[:, None] * inv], axis=-1)
    cos = jnp.concatenate([jnp.ones((n_ctx, MLA_ROPE // 2), F32), jnp.cos(ang)], axis=0)
    sin = jnp.concatenate([jnp.zeros((n_ctx, MLA_ROPE // 2), F32), jnp.sin(ang)], axis=0)
    cs32 = jnp.concatenate([cos, cos], axis=-1)
    sn32 = jnp.concatenate([sin, sin], axis=-1)
    s = n_ctx + n_lat
    z32 = jnp.zeros((s, MLA_ROPE), F32)
    t1 = jnp.concatenate([cs32, sn32, jnp.zeros((s, LANES - 2 * MLA_ROPE), F32)], axis=-1)
    cs = jnp.concatenate([jnp.ones((s, MLA_NOPE), F32), cs32, z32], axis=-1)
    sn = jnp.concatenate([jnp.zeros((s, MLA_NOPE), F32), sn32, z32], axis=-1)
    return t1, cs, sn


def _small_lanes(if_vals, dt_vals):
    v = jnp.zeros((LANES,), F32)
    v = v.at[SM_IF:SM_IF + if_vals.shape[0]].set(if_vals)
    return v.at[SM_DT:SM_DT + dt_vals.shape[0]].set(dt_vals)


def kernel(x, c, ctx, c_ctx, w_ada, b_ada, norm1_g, w_in, mla_qnorm_g, mla_w_uq, mla_kvnorm_g, mla_w_ukv,
           ml_conv_w, ml_conv_b, ml_gate_b, ml_norm_g, ssd_conv_w, ssd_conv_b, ssd_dt_bias, ssd_a_log, ssd_d,
           ssd_norm_g, w_br_mla, w_br_ml, w_br_ssd, w_out, norm2_g, w_router, b_router, w_up, b_up, w_down,
           b_down, final_g):
    bsz, n_lat, d = x.shape
    n_ctx = ctx.shape[1]
    depth = w_in.shape[0]
    s = n_ctx + n_lat
    t = bsz * s
    tiles_per_b = s // ROW_TILE
    assert n_ctx == ROW_TILE and n_lat % ROW_TILE == 0 and d == D_MODEL

    cond = jnp.zeros((16, d), F32).at[:bsz].set(c).at[bsz].set(c_ctx)
    mod_all = _ada_call(cond, w_ada, b_ada)
    mod_lat = mod_all[:, :bsz].reshape(depth, bsz, 1, 6, d)
    mod_ctx = jnp.broadcast_to(mod_all[:, bsz].reshape(depth, 1, 1, 6, d), (depth, bsz, 1, 6, d))
    mod_tab = jnp.concatenate([mod_ctx, mod_lat], axis=2)

    wp, wst, wqm, wqr, wk, wv = _pack_weights(w_in, mla_w_uq, mla_w_ukv)
    emat = _place_matrix()
    vone = _value_ones_row()
    t1, cs, sn = _rope_tables(n_ctx, n_lat)
    conv_w = jnp.concatenate([ssd_conv_w, ml_conv_w], axis=-1)
    conv_w = jnp.concatenate([conv_w, jnp.zeros((depth, 8 - CONV_W, CV_W), F32)], axis=1)
    conv_b = jnp.concatenate([ssd_conv_b, ml_conv_b], axis=-1).reshape(depth, 1, CV_W)
    post = jnp.ones((1, CV_W), F32).at[:, CV_K:CV_K + ML_QK_W].set(ML_QK ** -0.5)
    wr = jnp.concatenate([w_router, jnp.zeros((depth, d, LANES - N_EXPERTS), F32)], axis=-1)
    br = jnp.concatenate([b_router, jnp.zeros((depth, LANES - N_EXPERTS), F32)], axis=-1)

    xall = jnp.concatenate([ctx, x], axis=1).reshape(t, d)
    n_rows_max = -(-(t * TOP_K + N_EXPERTS * (MOE_ROWS - 1)) // MOE_ROWS) * MOE_ROWS
    n_blocks = n_rows_max // MOE_ROWS

    for l in range(depth):
        q, k, vv, smc, smt, cv, v, o, z, g = _in_call(
            xall, mod_tab[l], norm1_g[l].reshape(1, d), wp[l], wst[l], mla_qnorm_g[l].reshape(1, -1), wqm[l],
            wqr[l], mla_kvnorm_g[l].reshape(1, -1), wk[l], wv[l], vone, emat, t1, cs, sn, tiles_per_b)
        cvo = _conv_call(cv.reshape(bsz, s, CV_W), conv_w[l], conv_b[l], post, n_ctx)
        att = _attn_call(q.reshape(bsz, s, -1), k.reshape(bsz, s, -1), vv.reshape(bsz, s, -1), n_ctx)
        bias = _small_lanes(ml_gate_b[l].reshape(-1), ssd_dt_bias[l].reshape(-1))
        a_vec = _small_lanes(jnp.zeros((4 * ML_HEADS,), F32), -jnp.exp(ssd_a_log[l].astype(F32)).reshape(-1))
        brow, bcol = bias.reshape(1, LANES), bias.reshape(LANES, 1)
        hf, hb, yf, yb = _scan_call(cvo, v.reshape(bsz, s, -1), smc, smt, brow, bcol, a_vec.reshape(1, LANES),
                                    a_vec.reshape(LANES, 1), n_ctx)
        xall, h2, logits = _out_call(
            xall, mod_tab[l], att.reshape(t, -1), hf.reshape(t, -1), hb.reshape(t, -1), yf.reshape(t, -1),
            yb.reshape(t, -1), cvo.reshape(t, CV_W), z, o, g, ml_norm_g[l].reshape(1, -1),
            jnp.repeat(ssd_d[l], SSD_P).reshape(1, -1), ssd_norm_g[l].reshape(1, -1),
            w_br_mla[l].astype(BF16), w_br_ml[l].astype(BF16), w_br_ssd[l].astype(BF16), w_out[l].astype(BF16),
            norm2_g[l].reshape(1, d), wr[l], br[l].reshape(1, LANES), tiles_per_b)

        e_pad, gate, rank_pad, cnt = _route_call(logits)
        counts = cnt[0, :N_EXPERTS].astype(jnp.int32)
        padded = (counts + MOE_ROWS - 1) // MOE_ROWS * MOE_ROWS
        pad_end = jnp.cumsum(padded)
        pad_start = pad_end - padded
        start_row = jnp.zeros((1, LANES), jnp.int32).at[0, :N_EXPERTS].set(pad_start)
        block_first = jnp.arange(n_blocks, dtype=jnp.int32) * MOE_ROWS
        block_e = jnp.minimum(jnp.sum(pad_end[None, :] <= block_first[:, None], axis=1), N_EXPERTS - 1)
        n_used = (pad_end[-1] // MOE_ROWS).astype(jnp.int32).reshape(1)
        dest = _dest_call(e_pad, rank_pad, start_row)
        dst3 = dest[:, :TOP_K].reshape(t // ROW_TILE, 1, ROW_TILE * TOP_K)
        fill_lo = jnp.concatenate([pad_start + counts, pad_end[-1:]])
        fill_hi = jnp.concatenate([pad_end, jnp.full((1,), n_rows_max, jnp.int32)])
        xs = _dispatch_call(fill_lo, fill_hi, dst3, h2, n_rows_max)
        y = _moe_call(block_e.astype(jnp.int32), n_used, xs, w_up, b_up, w_down, b_down, l)
        xall = _combine_call(dst3, xall, mod_tab[l], gate, y, tiles_per_b)

    return _final_call(xall.reshape(bsz, s, d), final_g.reshape(1, d), n_ctx)
```

```python
import functools
import math

import jax
import jax.numpy as jnp
import numpy as np
from jax import lax
from jax.experimental import pallas as pl
from jax.experimental.pallas import tpu as pltpu

F32 = jnp.float32
BF16 = jnp.bfloat16
HIGHEST = lax.Precision.HIGHEST

D_MODEL = 1024
GRID_W = 64
EPS = 1e-6
ROPE_BASE = 10000.0
CONV_W = 5
MLA_HEADS = 8
MLA_NOPE = 64
MLA_ROPE = 32
MLA_V = 64
MLA_Q_LORA = 384
MLA_KV_LORA = 256
MLA_SCALE = (MLA_NOPE + MLA_ROPE) ** -0.5
LOG2_E = math.log2(math.e)
ML_HEADS = 4
ML_QK = 64
ML_V = 128
ML_QK_W = ML_HEADS * ML_QK
ML_V_W = ML_HEADS * ML_V
SSD_HEADS = 16
SSD_P = 64
SSD_N = 128
SSD_GROUPS = 4
SSD_HPG = SSD_HEADS // SSD_GROUPS
SSD_INNER = SSD_HEADS * SSD_P
SSD_BC_W = SSD_GROUPS * SSD_N
N_EXPERTS = 32
TOP_K = 4
D_FF = 1024
SWIGLU_LIMIT = 7.0
SWIGLU_ALPHA = 1.702

LANES = 128
SUBLANES = 8
HEAD_PAD = 128
ROW_SLABS = D_MODEL // LANES
V_ONES_LANE = (MLA_V, 0)
VMEM_LIMIT = 56 * 1024 * 1024

ROW_TILE = 256
CHUNK = 128
MOE_ROWS = 256
OUT_SUB_ROWS = 128
ATTN_HEADS = 4
ATTN_AHEAD = 2
DMA_ISSUE_UNROLL = 8

OFF_Q = 0
OFF_KV = OFF_Q + MLA_Q_LORA
OFF_SM = OFF_KV + MLA_KV_LORA
OFF_CV = OFF_SM + LANES
CV_W = 2 * ML_QK_W + SSD_INNER + 2 * SSD_BC_W
OFF_V = OFF_CV + CV_W
OFF_O = OFF_V + ML_V_W
OFF_Z = OFF_O + ML_V_W
OFF_G = OFF_Z + SSD_INNER
IN_PACKED = OFF_G + 3 * D_MODEL
SM_KR = 0
SM_KROT = MLA_ROPE
SM_IF = 2 * MLA_ROPE
SM_DT = SM_IF + 4 * ML_HEADS
CV_X = 0
CV_B = CV_X + SSD_INNER
CV_C = CV_B + SSD_BC_W
CV_Q = CV_C + SSD_BC_W
CV_K = CV_Q + ML_QK_W
CV_SSD_W = CV_Q


def _params(sem, vmem=VMEM_LIMIT):
    return pltpu.CompilerParams(dimension_semantics=sem, vmem_limit_bytes=vmem)


def _dot(a, b):
    return jnp.dot(a, b, preferred_element_type=F32)


def _dot_nt(a, b):
    return lax.dot_general(a, b, (((1,), (1,)), ((), ())), preferred_element_type=F32)


def _dot_hi(a, b):
    return jnp.dot(a, b, preferred_element_type=F32, precision=HIGHEST)


def _sigmoid(x):
    return 0.5 * jnp.tanh(0.5 * x) + 0.5


def _softplus(x):
    return jnp.maximum(x, 0.0) + jnp.log(1.0 + jnp.exp(-jnp.abs(x)))


def _log_sigmoid(x):
    return jnp.minimum(x, 0.0) - jnp.log(1.0 + jnp.exp(-jnp.abs(x)))


def _rms(x, g):
    return x * lax.rsqrt(jnp.mean(x * x, axis=-1, keepdims=True) + EPS) * g


assert ROW_SLABS == SUBLANES


def _to_tiles(ref, val, first_row=0):
    n = val.shape[0]
    for s in range(ROW_SLABS):
        ref[pl.ds(first_row * ROW_SLABS + s, n, stride=ROW_SLABS), :] = val[:, s * LANES:(s + 1) * LANES]


def _from_tiles(ref, n):
    return jnp.concatenate([ref[pl.ds(s, n, stride=ROW_SLABS), :] for s in range(ROW_SLABS)], axis=1)


def _ada_kernel(c_ref, w_ref, b_ref, o_ref):
    c = c_ref[...]
    o_ref[0] = _dot_hi(c * _sigmoid(c), w_ref[0]) + b_ref[0]


def _ada_call(cond, w_ada, b_ada):
    depth, d, n = w_ada.shape
    tn = 1536
    return pl.pallas_call(
        _ada_kernel,
        grid=(depth, n // tn),
        in_specs=[pl.BlockSpec((cond.shape[0], d), lambda l, j: (0, 0)),
                  pl.BlockSpec((1, d, tn), lambda l, j: (l, 0, j)),
                  pl.BlockSpec((1, 1, tn), lambda l, j: (l, 0, j))],
        out_specs=pl.BlockSpec((1, cond.shape[0], tn), lambda l, j: (l, 0, j)),
        out_shape=jax.ShapeDtypeStruct((depth, cond.shape[0], n), F32),
        compiler_params=_params(("arbitrary", "arbitrary")),
        name="ada_mod",
    )(cond, w_ada, b_ada.reshape(depth, 1, n))


def _in_kernel(x_ref, mod_ref, g1_ref, w_ref, wst_ref, gq_ref, wqm_ref, wqr_ref, gkv_ref, wk_ref, wv_ref,
               vone_ref, e_ref, t1_ref, cs_ref, sn_ref,
               q_ref, k_ref, vv_ref, smc_ref, smt_ref, cv_ref, v_ref, o_ref, z_ref, g_ref):
    x = x_ref[...]
    mod = mod_ref[0, 0]
    hn = _rms(x, g1_ref[...]) * (1.0 + mod[1:2]) + mod[0:1]
    hb = hn.astype(BF16)

    def proj(lo, hi):
        return _dot(hb, w_ref[:, lo:hi])

    sm = proj(OFF_SM, OFF_CV)
    smc_ref[...] = sm
    smt_ref[...] = _dot_nt(wst_ref[...], hb)
    cv_ref[...] = proj(OFF_CV, OFF_V).astype(BF16)
    v_ref[...] = proj(OFF_V, OFF_O).astype(BF16)
    o_ref[...] = proj(OFF_O, OFF_Z).astype(BF16)
    z_ref[...] = proj(OFF_Z, OFF_G).astype(BF16)
    g_ref[...] = proj(OFF_G, IN_PACKED).astype(BF16)

    qn = _rms(proj(OFF_Q, OFF_KV), gq_ref[...]).astype(BF16)
    qm = _dot(qn, wqm_ref[...])
    qr = _dot(qn, wqr_ref[...])
    cs = cs_ref[...]
    sn = sn_ref[...]
    for h in range(MLA_HEADS):
        sl = slice(h * HEAD_PAD, (h + 1) * HEAD_PAD)
        q_ref[:, sl] = ((qm[:, sl] * cs + qr[:, sl] * sn) * (MLA_SCALE * LOG2_E)).astype(BF16)
    kvn = _rms(proj(OFF_KV, OFF_SM), gkv_ref[...]).astype(BF16)
    kro = (sm * t1_ref[...]).astype(BF16)
    k_ref[...] = (_dot(kvn, wk_ref[...]) + _dot(kro, e_ref[...])).astype(BF16)
    vv_ref[...] = (_dot(kvn, wv_ref[...]) + vone_ref[...]).astype(BF16)


def _in_call(x, mod, g1, wp, wst, gq, wqm, wqr, gkv, wk, wv, vone, emat, t1, cs, sn, tiles_per_b):
    t, d = x.shape
    n_tiles = t // ROW_TILE

    def const(shape):
        return pl.BlockSpec(shape, lambda i: (0,) * len(shape), pipeline_mode=pl.Buffered(1))

    def rows(w):
        return pl.BlockSpec((ROW_TILE, w), lambda i: (i, 0))

    def tab():
        return pl.BlockSpec((ROW_TILE, LANES), lambda i: (i % tiles_per_b, 0))

    widths = [MLA_HEADS * HEAD_PAD, MLA_HEADS * HEAD_PAD, MLA_HEADS * HEAD_PAD, LANES, None, CV_W, ML_V_W, ML_V_W,
              SSD_INNER, 3 * D_MODEL]
    dtypes = [BF16, BF16, BF16, F32, F32, BF16, BF16, BF16, BF16, BF16]
    out_specs, out_shape = [], []
    for w, dt in zip(widths, dtypes):
        if w is None:
            out_specs.append(pl.BlockSpec((LANES, ROW_TILE), lambda i: (0, i)))
            out_shape.append(jax.ShapeDtypeStruct((LANES, t), dt))
        else:
            out_specs.append(rows(w))
            out_shape.append(jax.ShapeDtypeStruct((t, w), dt))
    return pl.pallas_call(
        _in_kernel,
        grid=(n_tiles,),
        in_specs=[rows(d),
                  pl.BlockSpec((1, 1, 6, d), lambda i: (i // tiles_per_b, jnp.minimum(i % tiles_per_b, 1), 0, 0)),
                  const((1, d)), const(wp.shape), const(wst.shape), const((1, MLA_Q_LORA)), const(wqm.shape),
                  const(wqr.shape), const((1, MLA_KV_LORA)), const(wk.shape), const(wv.shape), const(vone.shape),
                  const(emat.shape), tab(), tab(), tab()],
        out_specs=out_specs,
        out_shape=out_shape,
        compiler_params=_params(("arbitrary",)),
        name="in_proj",
    )(x, mod, g1, wp, wst, gq, wqm, wqr, gkv, wk, wv, vone, emat, t1, cs, sn)


def _conv_kernel(x_ref, w_ref, b_ref, s_ref, o_ref, *, n_ctx):
    x = x_ref[0].astype(F32)
    s = x.shape[0]
    t = lax.broadcasted_iota(jnp.int32, x.shape, 0)
    is_ctx = t < n_ctx
    pos = jnp.where(is_ctx, t, t - n_ctx)
    length = jnp.where(is_ctx, n_ctx, s - n_ctx)
    acc = jnp.zeros_like(x) + b_ref[...]
    for j in range(CONV_W):
        d = j - CONV_W // 2
        xs = x if d == 0 else pltpu.roll(x, (-d) % s, 0)
        valid = (pos + d >= 0) & (pos + d < length)
        acc = acc + jnp.where(valid, xs, 0.0) * w_ref[j:j + 1, :]
    o_ref[0] = (acc * _sigmoid(acc) * s_ref[...]).astype(o_ref.dtype)


def _conv_call(cv, w, b, post, n_ctx):
    bsz, s, c = cv.shape
    return pl.pallas_call(
        functools.partial(_conv_kernel, n_ctx=n_ctx),
        grid=(bsz, c // LANES),
        in_specs=[pl.BlockSpec((1, s, LANES), lambda bi, j: (bi, 0, j)),
                  pl.BlockSpec((8, LANES), lambda bi, j: (0, j)),
                  pl.BlockSpec((1, LANES), lambda bi, j: (0, j)),
                  pl.BlockSpec((1, LANES), lambda bi, j: (0, j))],
        out_specs=pl.BlockSpec((1, s, LANES), lambda bi, j: (bi, 0, j)),
        out_shape=jax.ShapeDtypeStruct(cv.shape, BF16),
        compiler_params=_params(("arbitrary", "arbitrary")),
        name="dwconv_silu",
    )(cv, w, b, post)


def _attn_kernel(q_ref, k_ref, v_ref, o_ref, *, n_ctx):
    qi = pl.program_id(2)

    def attend(n_keys):
        heads = [slice(j * HEAD_PAD, (j + 1) * HEAD_PAD) for j in range(ATTN_HEADS)]

        def scores(j):
            return _dot_nt(q_ref[0, :, heads[j]], k_ref[0, 0:n_keys, heads[j]])

        pending = [scores(j) for j in range(ATTN_AHEAD)]
        outs = []
        for j in range(ATTN_HEADS):
            s = pending.pop(0)
            p = jnp.exp2(s - jnp.max(s, axis=-1, keepdims=True))
            if j + ATTN_AHEAD < ATTN_HEADS:
                pending.append(scores(j + ATTN_AHEAD))
            pv = _dot(p.astype(BF16), v_ref[0, 0:n_keys, heads[j]])
            ones = V_ONES_LANE[j % 2]
            outs.append(pv / pv[:, ones:ones + 1])
        lane = lax.broadcasted_iota(jnp.int32, outs[0].shape, 1)
        for pp in range(ATTN_HEADS // 2):
            o_ref[0, :, pp * LANES:(pp + 1) * LANES] = jnp.where(lane < MLA_V, outs[2 * pp],
                                                                 outs[2 * pp + 1]).astype(o_ref.dtype)

    @pl.when(qi == 0)
    def _():
        attend(n_ctx)

    @pl.when(qi != 0)
    def _():
        attend(k_ref.shape[1])


def _attn_call(q, k, v, n_ctx):
    bsz, s, _ = q.shape
    assert n_ctx == ROW_TILE
    w_in = ATTN_HEADS * HEAD_PAD
    return pl.pallas_call(
        functools.partial(_attn_kernel, n_ctx=n_ctx),
        grid=(bsz, MLA_HEADS // ATTN_HEADS, s // ROW_TILE),
        in_specs=[pl.BlockSpec((1, ROW_TILE, w_in), lambda b, h, i: (b, i, h)),
                  pl.BlockSpec((1, s, w_in), lambda b, h, i: (b, 0, h)),
                  pl.BlockSpec((1, s, w_in), lambda b, h, i: (b, 0, h))],
        out_specs=pl.BlockSpec((1, ROW_TILE, ATTN_HEADS * MLA_V), lambda b, h, i: (b, i, h)),
        out_shape=jax.ShapeDtypeStruct((bsz, s, MLA_HEADS * MLA_V), BF16),
        compiler_params=_params(("arbitrary", "arbitrary", "arbitrary")),
        name="mla_attention",
    )(q, k, v)


def _tri(n, upper):
    r = lax.broadcasted_iota(jnp.int32, (n, n), 0)
    c = lax.broadcasted_iota(jnp.int32, (n, n), 1)
    return (r <= c) if upper else (r >= c)


def _bwd_chunk(j, nc_ctx, nc):
    return jnp.where(j < nc_ctx, nc_ctx - 1 - j, nc + nc_ctx - 1 - j)


def _mlstm_body(qkf_ref, vf_ref, gcf_ref, grf_ref, qkb_ref, vb_ref, gcb_ref, grb_ref, brow_ref, bcol_ref,
                hf_ref, hb_ref, st_ref, m_ref):
    L = CHUNK
    lane = lax.broadcasted_iota(jnp.int32, (L, LANES), 1)
    ones_col = jnp.where(lane == 0, 1.0, 0.0).astype(F32)
    lane_qk = lax.broadcasted_iota(jnp.int32, (L, ML_QK_W), 1)

    dirs = []
    for d, (qk_ref, v_ref, gc_ref, gr_ref, out_ref) in enumerate(
            [(qkf_ref, vf_ref, gcf_ref, grf_ref, hf_ref), (qkb_ref, vb_ref, gcb_ref, grb_ref, hb_ref)]):
        rev = d == 1
        gc = gc_ref[...] + brow_ref[...]
        gr = gr_ref[...] + bcol_ref[...]
        k_all = qk_ref[0, :, ML_QK_W:2 * ML_QK_W]
        dirs.append(dict(
            gc=gc, gr=gr, v_ref=v_ref, out_ref=out_ref, k_all=k_all, q_all=qk_ref[0, :, 0:ML_QK_W],
            fc_all=_dot_hi(jnp.where(_tri(L, rev), 1.0, 0.0).astype(F32), _log_sigmoid(gc)),
            fr_all=_dot_hi(_log_sigmoid(gr), jnp.where(_tri(L, not rev), 1.0, 0.0).astype(F32)),
            mask=_tri(L, rev),
            end=0 if rev else L - 1,
            kt_all=jnp.transpose(k_all.astype(F32)).astype(BF16),
            state=st_ref[d]))
    probs = [(d, h) for d in range(2) for h in range(ML_HEADS)]

    def stage_gates(d, h):
        dd = dirs[d]
        li_lane = SM_IF + d * 2 * ML_HEADS + h
        lf_lane = li_lane + ML_HEADS
        fc = dd['fc_all'][:, lf_lane:lf_lane + 1]
        fr = dd['fr_all'][lf_lane:lf_lane + 1, :]
        ic = dd['gc'][:, li_lane:li_lane + 1]
        ir = dd['gr'][li_lane:li_lane + 1, :]
        last = fc[dd['end']:dd['end'] + 1, :]
        m_old = m_ref[d * ML_HEADS + h][0:1, 0:1]
        m_new = jnp.maximum(last + m_old, jnp.max(last - fr + ir, axis=-1, keepdims=True))
        dmat = jnp.where(dd['mask'], fc - fr + ir, -jnp.inf)
        g = fc + m_old
        return dict(m_new=m_new, w_src=jnp.exp(last - fc + ic - m_new), w_old=jnp.exp(last + m_old - m_new),
                    dmat=dmat, g=g, m_row=jnp.maximum(g, jnp.max(dmat, axis=-1, keepdims=True)))

    gates = [stage_gates(d, h) for d, h in probs]
    q_hs = [jnp.where((lane_qk >= h * ML_QK) & (lane_qk < (h + 1) * ML_QK), dirs[d]['q_all'], 0).astype(BF16)
            for d, h in probs]
    qk = [_dot_nt(q_hs[i], dirs[d]['k_all']) for i, (d, h) in enumerate(probs)]
    qc = [_dot(q_hs[i], dirs[d]['state'].astype(BF16)) for i, (d, h) in enumerate(probs)]
    s = [qk[i] * jnp.exp(gates[i]['dmat'] - gates[i]['m_row']) for i in range(len(probs))]
    v_ext = [jnp.concatenate([dirs[d]['v_ref'][0, :, h * ML_V:(h + 1) * ML_V].astype(F32), ones_col], axis=1)
             for d, h in probs]
    tot = [_dot(s[i].astype(BF16), v_ext[i].astype(BF16)) + jnp.exp(gates[i]['g'] - gates[i]['m_row']) * qc[i]
           for i in range(len(probs))]
    upd = [_dot(dirs[d]['kt_all'][h * ML_QK:(h + 1) * ML_QK, :], (gates[i]['w_src'] * v_ext[i]).astype(BF16))
           for i, (d, h) in enumerate(probs)]
    for i, (d, h) in enumerate(probs):
        den = tot[i][:, ML_V:ML_V + 1]
        dirs[d]['out_ref'][0, :, h * ML_V:(h + 1) * ML_V] = (
            tot[i][:, :ML_V] / jnp.maximum(jnp.abs(den), jnp.exp(-gates[i]['m_row']))).astype(F32)
        rows = slice(h * ML_QK, (h + 1) * ML_QK)
        st_ref[d, rows, :] = gates[i]['w_old'] * dirs[d]['state'][rows, :] + upd[i]
        m_ref[d * ML_HEADS + h] = jnp.broadcast_to(gates[i]['m_new'], m_ref.shape[1:])


def _ssd_body(cvf_ref, gcf_ref, grf_ref, cvb_ref, gcb_ref, grb_ref, brow_ref, bcol_ref, arow_ref, acol_ref,
              yf_ref, yb_ref, st_ref):
    L = CHUNK
    gw = SSD_HPG * SSD_P
    lane_g = lax.broadcasted_iota(jnp.int32, (L, gw), 1)

    def per_head(cols):
        out = cols[SSD_HPG - 1]
        for i in range(SSD_HPG - 2, -1, -1):
            out = jnp.where(lane_g[0:cols[0].shape[0]] < (i + 1) * SSD_P, cols[i], out)
        return out

    dirs = []
    for d, (cv_ref, gc_ref, gr_ref, out_ref) in enumerate(
            [(cvf_ref, gcf_ref, grf_ref, yf_ref), (cvb_ref, gcb_ref, grb_ref, yb_ref)]):
        rev = d == 1
        dtc = _softplus(gc_ref[...] + brow_ref[...])
        dtr = _softplus(gr_ref[...] + bcol_ref[...])
        dirs.append(dict(
            cv_ref=cv_ref, out_ref=out_ref, dtc=dtc, dtr=dtr, mask=_tri(L, rev), end=0 if rev else L - 1,
            ac_all=_dot_hi(jnp.where(_tri(L, rev), 1.0, 0.0).astype(F32), dtc * arow_ref[...]),
            ar_all=_dot_hi(dtr * acol_ref[...], jnp.where(_tri(L, not rev), 1.0, 0.0).astype(F32))))
    groups = [(d, gi) for d in range(2) for gi in range(SSD_GROUPS)]

    def stage_group(d, gi):
        cv_ref = dirs[d]['cv_ref']
        x_g = cv_ref[0, :, CV_X + gi * gw:CV_X + (gi + 1) * gw]
        b_g = cv_ref[0, :, CV_B + gi * SSD_N:CV_B + (gi + 1) * SSD_N]
        c_g = cv_ref[0, :, CV_C + gi * SSD_N:CV_C + (gi + 1) * SSD_N]
        state = st_ref[d, gi]
        return dict(x_g=x_g, state=state, cb=_dot_nt(c_g, b_g), ch=_dot(c_g, state.astype(BF16)),
                    bt=jnp.transpose(b_g.astype(F32)).astype(BF16))

    def stage_head(d, gi, hg, grp):
        dd = dirs[d]
        ln = SM_DT + d * SSD_HEADS + gi * SSD_HPG + hg
        ac = dd['ac_all'][:, ln:ln + 1]
        ar = dd['ar_all'][ln:ln + 1, :]
        last = ac[dd['end']:dd['end'] + 1, :]
        seg = jnp.exp(jnp.where(dd['mask'], ac - ar, -jnp.inf))
        return dict(mm=(seg * grp['cb'] * dd['dtr'][ln:ln + 1, :]).astype(BF16), e=jnp.exp(ac),
                    w=jnp.exp(last - ac) * dd['dtc'][:, ln:ln + 1], last=jnp.exp(last))

    grps = [stage_group(d, gi) for d, gi in groups]
    heads = [[stage_head(d, gi, hg, grps[i]) for hg in range(SSD_HPG)] for i, (d, gi) in enumerate(groups)]
    ys = [[_dot(hd['mm'], grps[i]['x_g']) for hd in heads[i]] for i in range(len(groups))]
    for i, (d, gi) in enumerate(groups):
        grp = grps[i]
        y = ys[i][SSD_HPG - 1]
        for hg in range(SSD_HPG - 2, -1, -1):
            y = jnp.where(lane_g < (hg + 1) * SSD_P, ys[i][hg], y)
        y = y + grp['ch'] * per_head([hd['e'] for hd in heads[i]])
        dirs[d]['out_ref'][0, :, gi * gw:(gi + 1) * gw] = y
        xw = (grp['x_g'].astype(F32) * per_head([hd['w'] for hd in heads[i]])).astype(BF16)
        st_ref[d, gi] = per_head([hd['last'] for hd in heads[i]]) * grp['state'] + _dot(grp['bt'], xw)


def _mlstm_kernel(*refs):
    @pl.when(pl.program_id(1) == 0)
    def _():
        refs[-2][...] = jnp.zeros_like(refs[-2])
        refs[-1][...] = jnp.zeros_like(refs[-1])

    _mlstm_body(*refs)


def _ssd_kernel(*refs):
    @pl.when(pl.program_id(1) == 0)
    def _():
        refs[-1][...] = jnp.zeros_like(refs[-1])

    _ssd_body(*refs)


def _scan_call(cvo, v, smc, smt, brow, bcol, arow, acol, n_ctx):
    bsz, s, _ = cvo.shape
    nc, nc_ctx = s // CHUNK, n_ctx // CHUNK
    qk_w = 2 * ML_QK_W

    def fwd(b, j):
        return j

    def bwd(b, j):
        return _bwd_chunk(j, nc_ctx, nc)

    def gate_specs(ch):
        return [pl.BlockSpec((CHUNK, LANES), lambda b, j: (b * nc + ch(b, j), 0)),
                pl.BlockSpec((LANES, CHUNK), lambda b, j: (0, b * nc + ch(b, j)))]

    def ml_specs(ch):
        return [pl.BlockSpec((1, CHUNK, qk_w), lambda b, j: (b, ch(b, j), CV_Q // qk_w)),
                pl.BlockSpec((1, CHUNK, ML_V_W), lambda b, j: (b, ch(b, j), 0))] + gate_specs(ch)

    def ssd_specs(ch):
        return [pl.BlockSpec((1, CHUNK, CV_SSD_W), lambda b, j: (b, ch(b, j), 0))] + gate_specs(ch)

    def const(shape):
        return pl.BlockSpec(shape, lambda b, j: (0, 0))

    def out_spec(ch, w):
        return pl.BlockSpec((1, CHUNK, w), lambda b, j: (b, ch(b, j), 0))

    h_out = jax.ShapeDtypeStruct((bsz, s, ML_V_W), F32)
    y_out = jax.ShapeDtypeStruct((bsz, s, SSD_INNER), F32)
    hf, hb = pl.pallas_call(
        _mlstm_kernel,
        grid=(bsz, nc),
        in_specs=ml_specs(fwd) + ml_specs(bwd) + [const((1, LANES)), const((LANES, 1))],
        out_specs=[out_spec(fwd, ML_V_W), out_spec(bwd, ML_V_W)],
        out_shape=[h_out, h_out],
        scratch_shapes=[pltpu.VMEM((2, ML_QK_W, ML_V + LANES), F32),
                        pltpu.VMEM((2 * ML_HEADS, 8, LANES), F32)],
        compiler_params=_params(("arbitrary", "arbitrary")),
        name="mlstm_scan",
    )(cvo, v, smc, smt, cvo, v, smc, smt, brow, bcol)
    yf, yb = pl.pallas_call(
        _ssd_kernel,
        grid=(bsz, nc),
        in_specs=ssd_specs(fwd) + ssd_specs(bwd) + [const((1, LANES)), const((LANES, 1)), const((1, LANES)),
                                                    const((LANES, 1))],
        out_specs=[out_spec(fwd, SSD_INNER), out_spec(bwd, SSD_INNER)],
        out_shape=[y_out, y_out],
        scratch_shapes=[pltpu.VMEM((2, SSD_GROUPS, SSD_N, SSD_HPG * SSD_P), F32)],
        compiler_params=_params(("arbitrary", "arbitrary")),
        name="ssd_scan",
    )(cvo, smc, smt, cvo, smc, smt, brow, bcol, arow, acol)
    return hf, hb, yf, yb


def _out_kernel(x_ref, mod_ref, att_ref, hf_ref, hb_ref, yf_ref, yb_ref, sx_ref, z_ref, o_ref, g_ref,
                mlg_ref, sd_ref, sg_ref, wa_ref, wm_ref, ws_ref, wo_ref, g2_ref, wr_ref, br_ref,
                xo_ref, h2_ref, lg_ref):
    mod = mod_ref[0, 0]
    d = D_MODEL
    gw = SSD_HPG * SSD_P
    n_sub = ROW_TILE // OUT_SUB_ROWS
    subs = [slice(r * OUT_SUB_ROWS, (r + 1) * OUT_SUB_ROWS) for r in range(n_sub)]

    def branch_inputs(rs):
        hm = hf_ref[rs, :] + hb_ref[rs, :]
        m_out = jnp.concatenate([_rms(hm[:, h * ML_V:(h + 1) * ML_V], mlg_ref[:, h * ML_V:(h + 1) * ML_V])
                                 for h in range(ML_HEADS)], axis=1) * _sigmoid(o_ref[rs, :].astype(F32))
        z = z_ref[rs, :].astype(F32)
        y = (yf_ref[rs, :] + yb_ref[rs, :] + sd_ref[...] * sx_ref[rs, :].astype(F32)) * (z * _sigmoid(z))
        s_out = jnp.concatenate([_rms(y[:, gi * gw:(gi + 1) * gw], sg_ref[:, gi * gw:(gi + 1) * gw])
                                 for gi in range(SSD_GROUPS)], axis=1)
        return m_out.astype(BF16), s_out.astype(BF16)

    ins = [branch_inputs(rs) for rs in subs]
    dots = [(_dot(att_ref[rs, :], wa_ref[...]), _dot(ins[r][0], wm_ref[...]), _dot(ins[r][1], ws_ref[...]))
            for r, rs in enumerate(subs)]
    merged = []
    for r, rs in enumerate(subs):
        gt = g_ref[rs, :].astype(F32)
        merged.append((_sigmoid(gt[:, 0:d]) * dots[r][0] + _sigmoid(gt[:, d:2 * d]) * dots[r][1]
                       + _sigmoid(gt[:, 2 * d:3 * d]) * dots[r][2]).astype(BF16))
    outs = [_dot(m, wo_ref[...]) for m in merged]
    h2s = []
    for r, rs in enumerate(subs):
        xn = x_ref[rs, :] + mod[2:3] * outs[r]
        xo_ref[rs, :] = xn
        h2 = _rms(xn, g2_ref[...]) * (1.0 + mod[4:5]) + mod[3:4]
        _to_tiles(h2_ref, h2, first_row=r * OUT_SUB_ROWS)
        h2s.append(h2)
    for r, rs in enumerate(subs):
        lg_ref[rs, :] = _dot_hi(h2s[r], wr_ref[...]) + br_ref[...]


def _out_call(x, mod, att, hf, hb, yf, yb, cvo, z, o, g, mlg, sd, sg, wa, wm, ws, wo, g2, wr, br, tiles_per_b):
    t, d = x.shape
    n_tiles = t // ROW_TILE

    def const(shape):
        return pl.BlockSpec(shape, lambda i: (0,) * len(shape), pipeline_mode=pl.Buffered(1))

    def rows(w, blk=0):
        return pl.BlockSpec((ROW_TILE, w), lambda i: (i, blk))

    return pl.pallas_call(
        _out_kernel,
        grid=(n_tiles,),
        in_specs=[rows(d),
                  pl.BlockSpec((1, 1, 6, d), lambda i: (i // tiles_per_b, jnp.minimum(i % tiles_per_b, 1), 0, 0)),
                  rows(MLA_HEADS * MLA_V), rows(ML_V_W), rows(ML_V_W), rows(SSD_INNER), rows(SSD_INNER),
                  rows(SSD_INNER, CV_X // SSD_INNER), rows(SSD_INNER), rows(ML_V_W), rows(3 * d),
                  const((1, ML_V_W)), const((1, SSD_INNER)), const((1, SSD_INNER)),
                  const(wa.shape), const(wm.shape), const(ws.shape), const(wo.shape), const((1, d)),
                  const(wr.shape), const((1, LANES))],
        out_specs=[rows(d), pl.BlockSpec((ROW_TILE * ROW_SLABS, LANES), lambda i: (i, 0)), rows(LANES)],
        out_shape=[jax.ShapeDtypeStruct((t, d), F32), jax.ShapeDtypeStruct((t * ROW_SLABS, LANES), F32),
                   jax.ShapeDtypeStruct((t, LANES), F32)],
        compiler_params=_params(("arbitrary",)),
        name="mix_out",
    )(x, mod, att, hf, hb, yf, yb, cvo, z, o, g, mlg, sd, sg, wa, wm, ws, wo, g2, wr, br)


def _route_kernel(lg_ref, e_ref, gate_ref, rank_ref, cnt_ref, base_ref):
    i = pl.program_id(0)

    @pl.when(i == 0)
    def _():
        base_ref[...] = jnp.zeros_like(base_ref)

    v = lg_ref[...]
    n = v.shape[0]
    lane = lax.broadcasted_iota(jnp.int32, v.shape, 1)
    v = jnp.where(lane < N_EXPERTS, v, -jnp.inf)
    tops, hots = [], []
    e_out = jnp.zeros(v.shape, jnp.int32)
    for k in range(TOP_K):
        mk = jnp.max(v, axis=-1, keepdims=True)
        idx = jnp.min(jnp.where(v == mk, lane, LANES), axis=-1, keepdims=True)
        hot = lane == idx
        v = jnp.where(hot, -jnp.inf, v)
        tops.append(mk)
        hots.append(hot)
        e_out = jnp.where(lane == k, idx, e_out)
    ex = [jnp.exp(tk - tops[0]) for tk in tops]
    den = ex[0]
    for k in range(1, TOP_K):
        den = den + ex[k]
    gate = jnp.zeros(v.shape, F32)
    for k in range(TOP_K):
        gate = jnp.where(lane == k, ex[k] / den, gate)
    hot_sum = jnp.zeros(v.shape, F32)
    for k in range(TOP_K):
        hot_sum = hot_sum + jnp.where(hots[k], 1.0, 0.0)
    r = lax.broadcasted_iota(jnp.int32, (n, n), 0)
    c = lax.broadcasted_iota(jnp.int32, (n, n), 1)
    before = jnp.where(c < r, 1.0, 0.0).astype(BF16)
    base = base_ref[0:1, :]
    cum = _dot(before, hot_sum.astype(BF16)) + base
    rank = jnp.zeros(v.shape, jnp.int32)
    for k in range(TOP_K):
        rk = jnp.sum(jnp.where(hots[k], cum, 0.0), axis=-1, keepdims=True)
        rank = jnp.where(lane == k, rk.astype(jnp.int32), rank)
    e_ref[...] = e_out
    gate_ref[...] = gate
    rank_ref[...] = rank
    new_base = base + jnp.sum(hot_sum, axis=0, keepdims=True)
    base_ref[...] = jnp.broadcast_to(new_base, base_ref.shape)
    cnt_ref[...] = jnp.broadcast_to(new_base, cnt_ref.shape)


def _route_call(logits):
    t = logits.shape[0]

    def rows():
        return pl.BlockSpec((ROW_TILE, LANES), lambda i: (i, 0))

    return pl.pallas_call(
        _route_kernel,
        grid=(t // ROW_TILE,),
        in_specs=[rows()],
        out_specs=[rows(), rows(), rows(), pl.BlockSpec((8, LANES), lambda i: (0, 0))],
        out_shape=[jax.ShapeDtypeStruct((t, LANES), jnp.int32), jax.ShapeDtypeStruct((t, LANES), F32),
                   jax.ShapeDtypeStruct((t, LANES), jnp.int32), jax.ShapeDtypeStruct((8, LANES), F32)],
        scratch_shapes=[pltpu.VMEM((8, LANES), F32)],
        compiler_params=_params(("arbitrary",)),
        name="moe_route",
    )(logits)


def _dest_kernel(e_ref, rank_ref, start_ref, d_ref):
    e = e_ref[...]
    rank = rank_ref[...]
    lane = lax.broadcasted_iota(jnp.int32, e.shape, 1)
    out = jnp.zeros(e.shape, jnp.int32)
    for k in range(TOP_K):
        base = jnp.sum(jnp.where(lane == e[:, k:k + 1], start_ref[...], 0), axis=-1, keepdims=True)
        out = jnp.where(lane == k, (base + rank[:, k:k + 1]) * SUBLANES, out)
    d_ref[...] = out


def _dest_call(e_pad, rank_pad, start_row):
    t = e_pad.shape[0]

    def rows():
        return pl.BlockSpec((ROW_TILE, LANES), lambda i: (i, 0))

    return pl.pallas_call(
        _dest_kernel,
        grid=(t // ROW_TILE,),
        in_specs=[rows(), rows(), pl.BlockSpec((1, LANES), lambda i: (0, 0))],
        out_specs=rows(),
        out_shape=jax.ShapeDtypeStruct((t, LANES), jnp.int32),
        compiler_params=_params(("arbitrary",)),
        name="moe_dest",
    )(e_pad, rank_pad, start_row)


def _dispatch_kernel(lo_ref, hi_ref, dst_ref, h_ref, xs_hbm, stage, zrow, sem, zsem):
    i = pl.program_id(0)
    n = pl.num_programs(0)
    slot = i % 2
    stage[slot] = h_ref[...]

    def tile(ref, first):
        return ref.at[pl.ds(pl.multiple_of(first, SUBLANES), SUBLANES), :]

    def zero_copy(r):
        return pltpu.make_async_copy(zrow, tile(xs_hbm, r * SUBLANES), zsem)

    @pl.when(i == 0)
    def _():
        zrow[...] = jnp.zeros_like(zrow)
        for e in range(N_EXPERTS + 1):
            def start(r, carry):
                zero_copy(r).start()
                return carry
            lax.fori_loop(lo_ref[e], hi_ref[e], start, 0)
        for e in range(N_EXPERTS + 1):
            def wait(r, carry):
                zero_copy(r).wait()
                return carry
            lax.fori_loop(lo_ref[e], hi_ref[e], wait, 0)

    def body(r, carry):
        src = tile(stage.at[slot], r * SUBLANES)
        for k in range(TOP_K):
            pltpu.make_async_copy(src, tile(xs_hbm, dst_ref[0, 0, r * TOP_K + k]), sem.at[slot]).start()
        return carry
    lax.fori_loop(0, ROW_TILE, body, 0, unroll=DMA_ISSUE_UNROLL)

    def wait_step(s):
        for _ in range(TOP_K):
            pltpu.make_async_copy(stage.at[s], xs_hbm.at[pl.ds(0, ROW_TILE * SUBLANES), :], sem.at[s]).wait()

    @pl.when(i > 0)
    def _():
        wait_step(1 - slot)

    @pl.when(i == n - 1)
    def _():
        wait_step(slot)


def _dispatch_call(fill_lo, fill_hi, dst3, h2, n_rows):
    n_tiles = dst3.shape[0]
    grid_spec = pltpu.PrefetchScalarGridSpec(
        num_scalar_prefetch=2,
        grid=(n_tiles,),
        in_specs=[pl.BlockSpec((1, 1, ROW_TILE * TOP_K), lambda i, lo, hi: (i, 0, 0), memory_space=pltpu.SMEM),
                  pl.BlockSpec((ROW_TILE * ROW_SLABS, LANES), lambda i, lo, hi: (i, 0))],
        out_specs=pl.BlockSpec(memory_space=pl.ANY),
        scratch_shapes=[pltpu.VMEM((2, ROW_TILE * ROW_SLABS, LANES), F32), pltpu.VMEM((SUBLANES, LANES), F32),
                        pltpu.SemaphoreType.DMA((2,)), pltpu.SemaphoreType.DMA(())],
    )
    return pl.pallas_call(
        _dispatch_kernel,
        grid_spec=grid_spec,
        out_shape=jax.ShapeDtypeStruct((n_rows * ROW_SLABS, LANES), F32),
        compiler_params=_params(("arbitrary",)),
        name="moe_dispatch",
    )(fill_lo, fill_hi, dst3, h2)


def _moe_kernel(be_ref, nb_ref, x_ref, wu_ref, bu_ref, wd_ref, bd_ref, y_ref, wub, wdb):
    i = pl.program_id(0)

    @pl.when(i < nb_ref[0])
    def _():
        prev = be_ref[jnp.maximum(i - 1, 0)]

        @pl.when((i == 0) | (be_ref[i] != prev))
        def _():
            wub[...] = wu_ref[...].astype(BF16)
            wdb[...] = wd_ref[...].astype(BF16)

        gu = _dot(_from_tiles(x_ref, MOE_ROWS).astype(BF16), wub[...]) + bu_ref[...]
        glu = jnp.minimum(gu[:, :D_FF], SWIGLU_LIMIT)
        lin = jnp.clip(gu[:, D_FF:], -SWIGLU_LIMIT, SWIGLU_LIMIT)
        act = glu * _sigmoid(SWIGLU_ALPHA * glu) * (lin + 1.0)
        _to_tiles(y_ref, _dot(act.astype(BF16), wdb[...]) + bd_ref[...])

    @pl.when(i >= nb_ref[0])
    def _():
        y_ref[...] = jnp.zeros_like(y_ref)


def _moe_call(block_e, n_used, xs, w_up, b_up, w_down, b_down, layer):
    n_blocks = block_e.shape[0]
    d = D_MODEL
    blk_rows = MOE_ROWS * ROW_SLABS
    b_up = b_up.reshape(b_up.shape[0], N_EXPERTS, 1, 2 * D_FF)
    b_down = b_down.reshape(b_down.shape[0], N_EXPERTS, 1, d)

    def blk(i, nb):
        return jnp.minimum(i, nb[0] - 1)

    def wspec(shape):
        return pl.BlockSpec((None, None) + shape, lambda i, be, nb: (layer, be[blk(i, nb)], 0, 0))

    grid_spec = pltpu.PrefetchScalarGridSpec(
        num_scalar_prefetch=2,
        grid=(n_blocks,),
        in_specs=[pl.BlockSpec((blk_rows, LANES), lambda i, be, nb: (blk(i, nb), 0)),
                  wspec((d, 2 * D_FF)), wspec((1, 2 * D_FF)), wspec((D_FF, d)), wspec((1, d))],
        out_specs=pl.BlockSpec((blk_rows, LANES), lambda i, be, nb: (i, 0)),
        scratch_shapes=[pltpu.VMEM((d, 2 * D_FF), BF16), pltpu.VMEM((D_FF, d), BF16)],
    )
    return pl.pallas_call(
        _moe_kernel,
        grid_spec=grid_spec,
        out_shape=jax.ShapeDtypeStruct((n_blocks * blk_rows, LANES), F32),
        compiler_params=_params(("arbitrary",)),
        name="moe_experts",
    )(block_e, n_used, xs, w_up, b_up, w_down, b_down)


def _combine_kernel(dst_ref, dstn_ref, x_ref, mod_ref, gate_ref, y_hbm, o_ref, ybuf, sem):
    i = pl.program_id(0)
    n = pl.num_programs(0)
    slot = i % 2

    def gather(dref, dst_slot):
        def body(r, carry):
            for k in range(TOP_K):
                first = pl.multiple_of(dref[0, 0, r * TOP_K + k], SUBLANES)
                pltpu.make_async_copy(y_hbm.at[pl.ds(first, SUBLANES), :],
                                      ybuf.at[dst_slot, k, pl.ds(pl.multiple_of(r * SUBLANES, SUBLANES), SUBLANES), :],
                                      sem.at[dst_slot]).start()
            return carry
        lax.fori_loop(0, ROW_TILE, body, 0, unroll=DMA_ISSUE_UNROLL)

    @pl.when(i == 0)
    def _():
        gather(dst_ref, 0)

    @pl.when(i + 1 < n)
    def _():
        gather(dstn_ref, 1 - slot)

    for k in range(TOP_K):
        pltpu.make_async_copy(y_hbm.at[pl.ds(0, ROW_TILE * SUBLANES), :], ybuf.at[slot, k], sem.at[slot]).wait()
    gate = gate_ref[...]
    f = gate[:, 0:1] * _from_tiles(ybuf.at[slot, 0], ROW_TILE)
    for k in range(1, TOP_K):
        f = f + gate[:, k:k + 1] * _from_tiles(ybuf.at[slot, k], ROW_TILE)
    o_ref[...] = x_ref[...] + mod_ref[0, 0][5:6] * f


def _combine_call(dst3, x, mod, gate, y, tiles_per_b):
    t, d = x.shape
    n_tiles = t // ROW_TILE
    return pl.pallas_call(
        _combine_kernel,
        grid=(n_tiles,),
        in_specs=[pl.BlockSpec((1, 1, ROW_TILE * TOP_K), lambda i: (i, 0, 0), memory_space=pltpu.SMEM),
                  pl.BlockSpec((1, 1, ROW_TILE * TOP_K), lambda i: (jnp.minimum(i + 1, n_tiles - 1), 0, 0),
                               memory_space=pltpu.SMEM),
                  pl.BlockSpec((ROW_TILE, d), lambda i: (i, 0)),
                  pl.BlockSpec((1, 1, 6, d), lambda i: (i // tiles_per_b, jnp.minimum(i % tiles_per_b, 1), 0, 0)),
                  pl.BlockSpec((ROW_TILE, LANES), lambda i: (i, 0)),
                  pl.BlockSpec(memory_space=pl.ANY)],
        out_specs=pl.BlockSpec((ROW_TILE, d), lambda i: (i, 0)),
        out_shape=jax.ShapeDtypeStruct((t, d), F32),
        scratch_shapes=[pltpu.VMEM((2, TOP_K, ROW_TILE * ROW_SLABS, LANES), F32),
                        pltpu.SemaphoreType.DMA((2,))],
        compiler_params=_params(("arbitrary",)),
        name="moe_combine",
    )(dst3, dst3, x, mod, gate, y)


def _final_kernel(x_ref, g_ref, o_ref):
    o_ref[0] = _rms(x_ref[0], g_ref[...])


def _final_call(x3, g, n_ctx):
    bsz, s, d = x3.shape
    skip = n_ctx // ROW_TILE
    return pl.pallas_call(
        _final_kernel,
        grid=(bsz, (s - n_ctx) // ROW_TILE),
        in_specs=[pl.BlockSpec((1, ROW_TILE, d), lambda b, i: (b, i + skip, 0)),
                  pl.BlockSpec((1, d), lambda b, i: (0, 0))],
        out_specs=pl.BlockSpec((1, ROW_TILE, d), lambda b, i: (b, i, 0)),
        out_shape=jax.ShapeDtypeStruct((bsz, s - n_ctx, d), F32),
        compiler_params=_params(("arbitrary", "arbitrary")),
        name="final_norm",
    )(x3, g)


def _rot_cols(w):
    half = MLA_ROPE // 2
    return jnp.concatenate([-w[..., half:], w[..., :half]], axis=-1)


def _pack_weights(w_in, mla_w_uq, mla_w_ukv):
    depth = w_in.shape[0]
    sizes = (MLA_Q_LORA, MLA_KV_LORA, MLA_ROPE, 2 * ML_QK_W, ML_V_W, ML_V_W, 4 * ML_HEADS, SSD_INNER,
             SSD_INNER + 2 * SSD_BC_W, 2 * SSD_HEADS, 3 * D_MODEL)
    idx = [int(v) for v in np.cumsum(sizes)[:-1]]
    w_q, w_kv, w_kr, w_qk, w_v, w_o, w_if, w_z, w_xbc, w_dt, w_g = jnp.split(w_in, idx, axis=-1)
    pad = jnp.zeros(w_in.shape[:2] + (LANES - SM_DT - 2 * SSD_HEADS,), w_in.dtype)
    small = jnp.concatenate([w_kr, _rot_cols(w_kr), w_if, w_dt, pad], axis=-1)
    wp = jnp.concatenate([w_q, w_kv, small, w_xbc, w_qk, w_v, w_o, w_z, w_g], axis=-1).astype(BF16)
    wst = jnp.swapaxes(small, 1, 2).astype(BF16)

    wq = mla_w_uq.reshape(depth, MLA_Q_LORA, MLA_HEADS, MLA_NOPE + MLA_ROPE)
    nope, rope = wq[..., :MLA_NOPE], wq[..., MLA_NOPE:]
    zr = jnp.zeros_like(rope)
    zn = jnp.zeros_like(nope)
    wqm = jnp.concatenate([nope, rope, zr], axis=-1).reshape(depth, MLA_Q_LORA, -1).astype(BF16)
    wqr = jnp.concatenate([zn, _rot_cols(rope), zr], axis=-1).reshape(depth, MLA_Q_LORA, -1).astype(BF16)
    wkv = mla_w_ukv.reshape(depth, MLA_KV_LORA, MLA_HEADS, MLA_NOPE + MLA_V)
    knope, val = wkv[..., :MLA_NOPE], wkv[..., MLA_NOPE:]
    wk = jnp.concatenate([knope, jnp.zeros_like(knope)], axis=-1).reshape(depth, MLA_KV_LORA, -1).astype(BF16)
    zv = jnp.zeros_like(val)
    val_even = jnp.concatenate([val, zv], axis=-1)[:, :, 0::2]
    val_odd = jnp.concatenate([zv, val], axis=-1)[:, :, 1::2]
    wv = jnp.stack([val_even, val_odd], axis=3).reshape(depth, MLA_KV_LORA, -1).astype(BF16)
    return wp, wst, wqm, wqr, wk, wv


def _value_ones_row():
    e = np.zeros((1, MLA_HEADS * HEAD_PAD), np.float32)
    for h in range(MLA_HEADS):
        e[0, h * HEAD_PAD + V_ONES_LANE[h % 2]] = 1.0
    return jnp.asarray(e)


def _place_matrix():
    e = np.zeros((LANES, MLA_HEADS * HEAD_PAD), np.float32)
    for h in range(MLA_HEADS):
        for jj in range(MLA_ROPE):
            e[SM_KR + jj, h * HEAD_PAD + MLA_NOPE + jj] = 1.0
            e[SM_KROT + jj, h * HEAD_PAD + MLA_NOPE + jj] = 1.0
    return jnp.asarray(e, BF16)


def _rope_tables(n_ctx, n_lat):
    rows = n_lat // GRID_W
    row = jnp.broadcast_to(jnp.arange(rows)[:, None], (rows, GRID_W)).reshape(-1)
    col = jnp.broadcast_to(jnp.arange(GRID_W)[None, :], (rows, GRID_W)).reshape(-1)
    n_freq = MLA_ROPE // 4
    inv = ROPE_BASE ** (-jnp.arange(n_freq, dtype=F32) / n_freq)
    ang = jnp.concatenate([row[:, None] * inv, col[:, None] * inv], axis=-1)
    cos = jnp.concatenate([jnp.ones((n_ctx, MLA_ROPE // 2), F32), jnp.cos(ang)], axis=0)
    sin = jnp.concatenate([jnp.zeros((n_ctx, MLA_ROPE // 2), F32), jnp.sin(ang)], axis=0)
    cs32 = jnp.concatenate([cos, cos], axis=-1)
    sn32 = jnp.concatenate([sin, sin], axis=-1)
    s = n_ctx + n_lat
    z32 = jnp.zeros((s, MLA_ROPE), F32)
    t1 = jnp.concatenate([cs32, sn32, jnp.zeros((s, LANES - 2 * MLA_ROPE), F32)], axis=-1)
    cs = jnp.concatenate([jnp.ones((s, MLA_NOPE), F32), cs32, z32], axis=-1)
    sn = jnp.concatenate([jnp.zeros((s, MLA_NOPE), F32), sn32, z32], axis=-1)
    return t1, cs, sn


def _small_lanes(if_vals, dt_vals):
    v = jnp.zeros((LANES,), F32)
    v = v.at[SM_IF:SM_IF + if_vals.shape[0]].set(if_vals)
    return v.at[SM_DT:SM_DT + dt_vals.shape[0]].set(dt_vals)


def kernel(x, c, ctx, c_ctx, w_ada, b_ada, norm1_g, w_in, mla_qnorm_g, mla_w_uq, mla_kvnorm_g, mla_w_ukv,
           ml_conv_w, ml_conv_b, ml_gate_b, ml_norm_g, ssd_conv_w, ssd_conv_b, ssd_dt_bias, ssd_a_log, ssd_d,
           ssd_norm_g, w_br_mla, w_br_ml, w_br_ssd, w_out, norm2_g, w_router, b_router, w_up, b_up, w_down,
           b_down, final_g):
    bsz, n_lat, d = x.shape
    n_ctx = ctx.shape[1]
    depth = w_in.shape[0]
    s = n_ctx + n_lat
    t = bsz * s
    tiles_per_b = s // ROW_TILE
    assert n_ctx == ROW_TILE and n_lat % ROW_TILE == 0 and d == D_MODEL

    cond = jnp.zeros((16, d), F32).at[:bsz].set(c).at[bsz].set(c_ctx)
    mod_all = _ada_call(cond, w_ada, b_ada)
    mod_lat = mod_all[:, :bsz].reshape(depth, bsz, 1, 6, d)
    mod_ctx = jnp.broadcast_to(mod_all[:, bsz].reshape(depth, 1, 1, 6, d), (depth, bsz, 1, 6, d))
    mod_tab = jnp.concatenate([mod_ctx, mod_lat], axis=2)

    wp, wst, wqm, wqr, wk, wv = _pack_weights(w_in, mla_w_uq, mla_w_ukv)
    emat = _place_matrix()
    vone = _value_ones_row()
    t1, cs, sn = _rope_tables(n_ctx, n_lat)
    conv_w = jnp.concatenate([ssd_conv_w, ml_conv_w], axis=-1)
    conv_w = jnp.concatenate([conv_w, jnp.zeros((depth, 8 - CONV_W, CV_W), F32)], axis=1)
    conv_b = jnp.concatenate([ssd_conv_b, ml_conv_b], axis=-1).reshape(depth, 1, CV_W)
    post = jnp.ones((1, CV_W), F32).at[:, CV_K:CV_K + ML_QK_W].set(ML_QK ** -0.5)
    wr = jnp.concatenate([w_router, jnp.zeros((depth, d, LANES - N_EXPERTS), F32)], axis=-1)
    br = jnp.concatenate([b_router, jnp.zeros((depth, LANES - N_EXPERTS), F32)], axis=-1)

    xall = jnp.concatenate([ctx, x], axis=1).reshape(t, d)
    n_rows_max = -(-(t * TOP_K + N_EXPERTS * (MOE_ROWS - 1)) // MOE_ROWS) * MOE_ROWS
    n_blocks = n_rows_max // MOE_ROWS

    for l in range(depth):
        q, k, vv, smc, smt, cv, v, o, z, g = _in_call(
            xall, mod_tab[l], norm1_g[l].reshape(1, d), wp[l], wst[l], mla_qnorm_g[l].reshape(1, -1), wqm[l],
            wqr[l], mla_kvnorm_g[l].reshape(1, -1), wk[l], wv[l], vone, emat, t1, cs, sn, tiles_per_b)
        cvo = _conv_call(cv.reshape(bsz, s, CV_W), conv_w[l], conv_b[l], post, n_ctx)
        att = _attn_call(q.reshape(bsz, s, -1), k.reshape(bsz, s, -1), vv.reshape(bsz, s, -1), n_ctx)
        bias = _small_lanes(ml_gate_b[l].reshape(-1), ssd_dt_bias[l].reshape(-1))
        a_vec = _small_lanes(jnp.zeros((4 * ML_HEADS,), F32), -jnp.exp(ssd_a_log[l].astype(F32)).reshape(-1))
        brow, bcol = bias.reshape(1, LANES), bias.reshape(LANES, 1)
        hf, hb, yf, yb = _scan_call(cvo, v.reshape(bsz, s, -1), smc, smt, brow, bcol, a_vec.reshape(1, LANES),
                                    a_vec.reshape(LANES, 1), n_ctx)
        xall, h2, logits = _out_call(
            xall, mod_tab[l], att.reshape(t, -1), hf.reshape(t, -1), hb.reshape(t, -1), yf.reshape(t, -1),
            yb.reshape(t, -1), cvo.reshape(t, CV_W), z, o, g, ml_norm_g[l].reshape(1, -1),
            jnp.repeat(ssd_d[l], SSD_P).reshape(1, -1), ssd_norm_g[l].reshape(1, -1),
            w_br_mla[l].astype(BF16), w_br_ml[l].astype(BF16), w_br_ssd[l].astype(BF16), w_out[l].astype(BF16),
            norm2_g[l].reshape(1, d), wr[l], br[l].reshape(1, LANES), tiles_per_b)

        e_pad, gate, rank_pad, cnt = _route_call(logits)
        counts = cnt[0, :N_EXPERTS].astype(jnp.int32)
        padded = (counts + MOE_ROWS - 1) // MOE_ROWS * MOE_ROWS
        pad_end = jnp.cumsum(padded)
        pad_start = pad_end - padded
        start_row = jnp.zeros((1, LANES), jnp.int32).at[0, :N_EXPERTS].set(pad_start)
        block_first = jnp.arange(n_blocks, dtype=jnp.int32) * MOE_ROWS
        block_e = jnp.minimum(jnp.sum(pad_end[None, :] <= block_first[:, None], axis=1), N_EXPERTS - 1)
        n_used = (pad_end[-1] // MOE_ROWS).astype(jnp.int32).reshape(1)
        dest = _dest_call(e_pad, rank_pad, start_row)
        dst3 = dest[:, :TOP_K].reshape(t // ROW_TILE, 1, ROW_TILE * TOP_K)
        fill_lo = jnp.concatenate([pad_start + counts, pad_end[-1:]])
        fill_hi = jnp.concatenate([pad_end, jnp.full((1,), n_rows_max, jnp.int32)])
        xs = _dispatch_call(fill_lo, fill_hi, dst3, h2, n_rows_max)
        y = _moe_call(block_e.astype(jnp.int32), n_used, xs, w_up, b_up, w_down, b_down, l)
        xall = _combine_call(dst3, xall, mod_tab[l], gate, y, tiles_per_b)

    return _final_call(xall.reshape(bsz, s, d), final_g.reshape(1, d), n_ctx)
```

```python
import functools
import math

import jax
import jax.numpy as jnp
import numpy as np
from jax import lax
from jax.experimental import pallas as pl
from jax.experimental.pallas import tpu as pltpu

F32 = jnp.float32
BF16 = jnp.bfloat16
HIGHEST = lax.Precision.HIGHEST

D_MODEL = 1024
GRID_W = 64
EPS = 1e-6
ROPE_BASE = 10000.0
CONV_W = 5
MLA_HEADS = 8
MLA_NOPE = 64
MLA_ROPE = 32
MLA_V = 64
MLA_Q_LORA = 384
MLA_KV_LORA = 256
MLA_SCALE = (MLA_NOPE + MLA_ROPE) ** -0.5
LOG2_E = math.log2(math.e)
ML_HEADS = 4
ML_QK = 64
ML_V = 128
ML_QK_W = ML_HEADS * ML_QK
ML_V_W = ML_HEADS * ML_V
SSD_HEADS = 16
SSD_P = 64
SSD_N = 128
SSD_GROUPS = 4
SSD_HPG = SSD_HEADS // SSD_GROUPS
SSD_INNER = SSD_HEADS * SSD_P
SSD_BC_W = SSD_GROUPS * SSD_N
N_EXPERTS = 32
TOP_K = 4
D_FF = 1024
SWIGLU_LIMIT = 7.0
SWIGLU_ALPHA = 1.702

LANES = 128
SUBLANES = 8
HEAD_PAD = 128
ROW_SLABS = D_MODEL // LANES
V_ONES_LANE = (MLA_V, 0)
VMEM_LIMIT = 56 * 1024 * 1024

ROW_TILE = 256
CHUNK = 256
MOE_ROWS = 256
OUT_SUB_ROWS = 128
ATTN_HEADS = 4
ATTN_AHEAD = 2
DMA_ISSUE_UNROLL = 8

OFF_Q = 0
OFF_KV = OFF_Q + MLA_Q_LORA
OFF_SM = OFF_KV + MLA_KV_LORA
OFF_CV = OFF_SM + LANES
CV_W = 2 * ML_QK_W + SSD_INNER + 2 * SSD_BC_W
OFF_V = OFF_CV + CV_W
OFF_O = OFF_V + ML_V_W
OFF_Z = OFF_O + ML_V_W
OFF_G = OFF_Z + SSD_INNER
IN_PACKED = OFF_G + 3 * D_MODEL
SM_KR = 0
SM_KROT = MLA_ROPE
SM_IF = 2 * MLA_ROPE
SM_DT = SM_IF + 4 * ML_HEADS
CV_X = 0
CV_B = CV_X + SSD_INNER
CV_C = CV_B + SSD_BC_W
CV_Q = CV_C + SSD_BC_W
CV_K = CV_Q + ML_QK_W
CV_SSD_W = CV_Q


def _params(sem, vmem=VMEM_LIMIT):
    return pltpu.CompilerParams(dimension_semantics=sem, vmem_limit_bytes=vmem)


def _dot(a, b):
    return jnp.dot(a, b, preferred_element_type=F32)


def _dot_nt(a, b):
    return lax.dot_general(a, b, (((1,), (1,)), ((), ())), preferred_element_type=F32)


def _dot_hi(a, b):
    return jnp.dot(a, b, preferred_element_type=F32, precision=HIGHEST)


def _sigmoid(x):
    return 0.5 * jnp.tanh(0.5 * x) + 0.5


def _softplus(x):
    return jnp.maximum(x, 0.0) + jnp.log(1.0 + jnp.exp(-jnp.abs(x)))


def _log_sigmoid(x):
    return jnp.minimum(x, 0.0) - jnp.log(1.0 + jnp.exp(-jnp.abs(x)))


def _rms(x, g):
    return x * lax.rsqrt(jnp.mean(x * x, axis=-1, keepdims=True) + EPS) * g


assert ROW_SLABS == SUBLANES


def _to_tiles(ref, val, first_row=0):
    n = val.shape[0]
    for s in range(ROW_SLABS):
        ref[pl.ds(first_row * ROW_SLABS + s, n, stride=ROW_SLABS), :] = val[:, s * LANES:(s + 1) * LANES]


def _from_tiles(ref, n):
    return jnp.concatenate([ref[pl.ds(s, n, stride=ROW_SLABS), :] for s in range(ROW_SLABS)], axis=1)


def _ada_kernel(c_ref, w_ref, b_ref, o_ref):
    c = c_ref[...]
    o_ref[0] = _dot_hi(c * _sigmoid(c), w_ref[0]) + b_ref[0]


def _ada_call(cond, w_ada, b_ada):
    depth, d, n = w_ada.shape
    tn = 1536
    return pl.pallas_call(
        _ada_kernel,
        grid=(depth, n // tn),
        in_specs=[pl.BlockSpec((cond.shape[0], d), lambda l, j: (0, 0)),
                  pl.BlockSpec((1, d, tn), lambda l, j: (l, 0, j)),
                  pl.BlockSpec((1, 1, tn), lambda l, j: (l, 0, j))],
        out_specs=pl.BlockSpec((1, cond.shape[0], tn), lambda l, j: (l, 0, j)),
        out_shape=jax.ShapeDtypeStruct((depth, cond.shape[0], n), F32),
        compiler_params=_params(("arbitrary", "arbitrary")),
        name="ada_mod",
    )(cond, w_ada, b_ada.reshape(depth, 1, n))


def _in_kernel(x_ref, mod_ref, g1_ref, w_ref, wst_ref, gq_ref, wqm_ref, wqr_ref, gkv_ref, wk_ref, wv_ref,
               vone_ref, e_ref, t1_ref, cs_ref, sn_ref,
               q_ref, k_ref, vv_ref, smc_ref, smt_ref, cv_ref, v_ref, o_ref, z_ref, g_ref):
    x = x_ref[...]
    mod = mod_ref[0, 0]
    hn = _rms(x, g1_ref[...]) * (1.0 + mod[1:2]) + mod[0:1]
    hb = hn.astype(BF16)

    def proj(lo, hi):
        return _dot(hb, w_ref[:, lo:hi])

    sm = proj(OFF_SM, OFF_CV)
    smc_ref[...] = sm
    smt_ref[...] = _dot_nt(wst_ref[...], hb)
    cv_ref[...] = proj(OFF_CV, OFF_V).astype(BF16)
    v_ref[...] = proj(OFF_V, OFF_O).astype(BF16)
    o_ref[...] = proj(OFF_O, OFF_Z).astype(BF16)
    z_ref[...] = proj(OFF_Z, OFF_G).astype(BF16)
    g_ref[...] = proj(OFF_G, IN_PACKED).astype(BF16)

    qn = _rms(proj(OFF_Q, OFF_KV), gq_ref[...]).astype(BF16)
    qm = _dot(qn, wqm_ref[...])
    qr = _dot(qn, wqr_ref[...])
    cs = cs_ref[...]
    sn = sn_ref[...]
    for h in range(MLA_HEADS):
        sl = slice(h * HEAD_PAD, (h + 1) * HEAD_PAD)
        q_ref[:, sl] = ((qm[:, sl] * cs + qr[:, sl] * sn) * (MLA_SCALE * LOG2_E)).astype(BF16)
    kvn = _rms(proj(OFF_KV, OFF_SM), gkv_ref[...]).astype(BF16)
    kro = (sm * t1_ref[...]).astype(BF16)
    k_ref[...] = (_dot(kvn, wk_ref[...]) + _dot(kro, e_ref[...])).astype(BF16)
    vv_ref[...] = (_dot(kvn, wv_ref[...]) + vone_ref[...]).astype(BF16)


def _in_call(x, mod, g1, wp, wst, gq, wqm, wqr, gkv, wk, wv, vone, emat, t1, cs, sn, tiles_per_b):
    t, d = x.shape
    n_tiles = t // ROW_TILE

    def const(shape):
        return pl.BlockSpec(shape, lambda i: (0,) * len(shape), pipeline_mode=pl.Buffered(1))

    def rows(w):
        return pl.BlockSpec((ROW_TILE, w), lambda i: (i, 0))

    def tab():
        return pl.BlockSpec((ROW_TILE, LANES), lambda i: (i % tiles_per_b, 0))

    widths = [MLA_HEADS * HEAD_PAD, MLA_HEADS * HEAD_PAD, MLA_HEADS * HEAD_PAD, LANES, None, CV_W, ML_V_W, ML_V_W,
              SSD_INNER, 3 * D_MODEL]
    dtypes = [BF16, BF16, BF16, F32, F32, BF16, BF16, BF16, BF16, BF16]
    out_specs, out_shape = [], []
    for w, dt in zip(widths, dtypes):
        if w is None:
            out_specs.append(pl.BlockSpec((LANES, ROW_TILE), lambda i: (0, i)))
            out_shape.append(jax.ShapeDtypeStruct((LANES, t), dt))
        else:
            out_specs.append(rows(w))
            out_shape.append(jax.ShapeDtypeStruct((t, w), dt))
    return pl.pallas_call(
        _in_kernel,
        grid=(n_tiles,),
        in_specs=[rows(d),
                  pl.BlockSpec((1, 1, 6, d), lambda i: (i // tiles_per_b, jnp.minimum(i % tiles_per_b, 1), 0, 0)),
                  const((1, d)), const(wp.shape), const(wst.shape), const((1, MLA_Q_LORA)), const(wqm.shape),
                  const(wqr.shape), const((1, MLA_KV_LORA)), const(wk.shape), const(wv.shape), const(vone.shape),
                  const(emat.shape), tab(), tab(), tab()],
        out_specs=out_specs,
        out_shape=out_shape,
        compiler_params=_params(("arbitrary",)),
        name="in_proj",
    )(x, mod, g1, wp, wst, gq, wqm, wqr, gkv, wk, wv, vone, emat, t1, cs, sn)


def _conv_kernel(x_ref, w_ref, b_ref, s_ref, o_ref, *, n_ctx):
    x = x_ref[0].astype(F32)
    s = x.shape[0]
    t = lax.broadcasted_iota(jnp.int32, x.shape, 0)
    is_ctx = t < n_ctx
    pos = jnp.where(is_ctx, t, t - n_ctx)
    length = jnp.where(is_ctx, n_ctx, s - n_ctx)
    acc = jnp.zeros_like(x) + b_ref[...]
    for j in range(CONV_W):
        d = j - CONV_W // 2
        xs = x if d == 0 else pltpu.roll(x, (-d) % s, 0)
        valid = (pos + d >= 0) & (pos + d < length)
        acc = acc + jnp.where(valid, xs, 0.0) * w_ref[j:j + 1, :]
    o_ref[0] = (acc * _sigmoid(acc) * s_ref[...]).astype(o_ref.dtype)


def _conv_call(cv, w, b, post, n_ctx):
    bsz, s, c = cv.shape
    return pl.pallas_call(
        functools.partial(_conv_kernel, n_ctx=n_ctx),
        grid=(bsz, c // LANES),
        in_specs=[pl.BlockSpec((1, s, LANES), lambda bi, j: (bi, 0, j)),
                  pl.BlockSpec((8, LANES), lambda bi, j: (0, j)),
                  pl.BlockSpec((1, LANES), lambda bi, j: (0, j)),
                  pl.BlockSpec((1, LANES), lambda bi, j: (0, j))],
        out_specs=pl.BlockSpec((1, s, LANES), lambda bi, j: (bi, 0, j)),
        out_shape=jax.ShapeDtypeStruct(cv.shape, BF16),
        compiler_params=_params(("arbitrary", "arbitrary")),
        name="dwconv_silu",
    )(cv, w, b, post)


def _attn_kernel(q_ref, k_ref, v_ref, o_ref, *, n_ctx):
    qi = pl.program_id(2)

    def attend(n_keys):
        heads = [slice(j * HEAD_PAD, (j + 1) * HEAD_PAD) for j in range(ATTN_HEADS)]

        def scores(j):
            return _dot_nt(q_ref[0, :, heads[j]], k_ref[0, 0:n_keys, heads[j]])

        pending = [scores(j) for j in range(ATTN_AHEAD)]
        outs = []
        for j in range(ATTN_HEADS):
            s = pending.pop(0)
            p = jnp.exp2(s - jnp.max(s, axis=-1, keepdims=True))
            if j + ATTN_AHEAD < ATTN_HEADS:
                pending.append(scores(j + ATTN_AHEAD))
            pv = _dot(p.astype(BF16), v_ref[0, 0:n_keys, heads[j]])
            ones = V_ONES_LANE[j % 2]
            outs.append(pv / pv[:, ones:ones + 1])
        lane = lax.broadcasted_iota(jnp.int32, outs[0].shape, 1)
        for pp in range(ATTN_HEADS // 2):
            o_ref[0, :, pp * LANES:(pp + 1) * LANES] = jnp.where(lane < MLA_V, outs[2 * pp],
                                                                 outs[2 * pp + 1]).astype(o_ref.dtype)

    @pl.when(qi == 0)
    def _():
        attend(n_ctx)

    @pl.when(qi != 0)
    def _():
        attend(k_ref.shape[1])


def _attn_call(q, k, v, n_ctx):
    bsz, s, _ = q.shape
    assert n_ctx == ROW_TILE
    w_in = ATTN_HEADS * HEAD_PAD
    return pl.pallas_call(
        functools.partial(_attn_kernel, n_ctx=n_ctx),
        grid=(bsz, MLA_HEADS // ATTN_HEADS, s // ROW_TILE),
        in_specs=[pl.BlockSpec((1, ROW_TILE, w_in), lambda b, h, i: (b, i, h)),
                  pl.BlockSpec((1, s, w_in), lambda b, h, i: (b, 0, h)),
                  pl.BlockSpec((1, s, w_in), lambda b, h, i: (b, 0, h))],
        out_specs=pl.BlockSpec((1, ROW_TILE, ATTN_HEADS * MLA_V), lambda b, h, i: (b, i, h)),
        out_shape=jax.ShapeDtypeStruct((bsz, s, MLA_HEADS * MLA_V), BF16),
        compiler_params=_params(("arbitrary", "arbitrary", "arbitrary")),
        name="mla_attention",
    )(q, k, v)


def _tri(n, upper):
    r = lax.broadcasted_iota(jnp.int32, (n, n), 0)
    c = lax.broadcasted_iota(jnp.int32, (n, n), 1)
    return (r <= c) if upper else (r >= c)


def _bwd_chunk(j, nc_ctx, nc):
    return jnp.where(j < nc_ctx, nc_ctx - 1 - j, nc + nc_ctx - 1 - j)


def _mlstm_body(qkf_ref, vf_ref, gcf_ref, grf_ref, qkb_ref, vb_ref, gcb_ref, grb_ref, brow_ref, bcol_ref,
                hf_ref, hb_ref, st_ref, m_ref):
    L = CHUNK
    lane = lax.broadcasted_iota(jnp.int32, (L, LANES), 1)
    ones_col = jnp.where(lane == 0, 1.0, 0.0).astype(F32)
    lane_qk = lax.broadcasted_iota(jnp.int32, (L, ML_QK_W), 1)

    dirs = []
    for d, (qk_ref, v_ref, gc_ref, gr_ref, out_ref) in enumerate(
            [(qkf_ref, vf_ref, gcf_ref, grf_ref, hf_ref), (qkb_ref, vb_ref, gcb_ref, grb_ref, hb_ref)]):
        rev = d == 1
        gc = gc_ref[...] + brow_ref[...]
        gr = gr_ref[...] + bcol_ref[...]
        k_all = qk_ref[0, :, ML_QK_W:2 * ML_QK_W]
        dirs.append(dict(
            gc=gc, gr=gr, v_ref=v_ref, out_ref=out_ref, k_all=k_all, q_all=qk_ref[0, :, 0:ML_QK_W],
            fc_all=_dot_hi(jnp.where(_tri(L, rev), 1.0, 0.0).astype(F32), _log_sigmoid(gc)),
            fr_all=_dot_hi(_log_sigmoid(gr), jnp.where(_tri(L, not rev), 1.0, 0.0).astype(F32)),
            mask=_tri(L, rev),
            end=0 if rev else L - 1,
            kt_all=jnp.transpose(k_all.astype(F32)).astype(BF16),
            state=st_ref[d]))
    probs = [(d, h) for d in range(2) for h in range(ML_HEADS)]

    def stage_gates(d, h):
        dd = dirs[d]
        li_lane = SM_IF + d * 2 * ML_HEADS + h
        lf_lane = li_lane + ML_HEADS
        fc = dd['fc_all'][:, lf_lane:lf_lane + 1]
        fr = dd['fr_all'][lf_lane:lf_lane + 1, :]
        ic = dd['gc'][:, li_lane:li_lane + 1]
        ir = dd['gr'][li_lane:li_lane + 1, :]
        last = fc[dd['end']:dd['end'] + 1, :]
        m_old = m_ref[d * ML_HEADS + h][0:1, 0:1]
        m_new = jnp.maximum(last + m_old, jnp.max(last - fr + ir, axis=-1, keepdims=True))
        dmat = jnp.where(dd['mask'], fc - fr + ir, -jnp.inf)
        g = fc + m_old
        return dict(m_new=m_new, w_src=jnp.exp(last - fc + ic - m_new), w_old=jnp.exp(last + m_old - m_new),
                    dmat=dmat, g=g, m_row=jnp.maximum(g, jnp.max(dmat, axis=-1, keepdims=True)))

    gates = [stage_gates(d, h) for d, h in probs]
    q_hs = [jnp.where((lane_qk >= h * ML_QK) & (lane_qk < (h + 1) * ML_QK), dirs[d]['q_all'], 0).astype(BF16)
            for d, h in probs]
    qk = [_dot_nt(q_hs[i], dirs[d]['k_all']) for i, (d, h) in enumerate(probs)]
    qc = [_dot(q_hs[i], dirs[d]['state'].astype(BF16)) for i, (d, h) in enumerate(probs)]
    s = [qk[i] * jnp.exp(gates[i]['dmat'] - gates[i]['m_row']) for i in range(len(probs))]
    v_ext = [jnp.concatenate([dirs[d]['v_ref'][0, :, h * ML_V:(h + 1) * ML_V].astype(F32), ones_col], axis=1)
             for d, h in probs]
    tot = [_dot(s[i].astype(BF16), v_ext[i].astype(BF16)) + jnp.exp(gates[i]['g'] - gates[i]['m_row']) * qc[i]
           for i in range(len(probs))]
    upd = [_dot(dirs[d]['kt_all'][h * ML_QK:(h + 1) * ML_QK, :], (gates[i]['w_src'] * v_ext[i]).astype(BF16))
           for i, (d, h) in enumerate(probs)]
    for i, (d, h) in enumerate(probs):
        den = tot[i][:, ML_V:ML_V + 1]
        dirs[d]['out_ref'][0, :, h * ML_V:(h + 1) * ML_V] = (
            tot[i][:, :ML_V] / jnp.maximum(jnp.abs(den), jnp.exp(-gates[i]['m_row']))).astype(F32)
        rows = slice(h * ML_QK, (h + 1) * ML_QK)
        st_ref[d, rows, :] = gates[i]['w_old'] * dirs[d]['state'][rows, :] + upd[i]
        m_ref[d * ML_HEADS + h] = jnp.broadcast_to(gates[i]['m_new'], m_ref.shape[1:])


def _ssd_body(cvf_ref, gcf_ref, grf_ref, cvb_ref, gcb_ref, grb_ref, brow_ref, bcol_ref, arow_ref, acol_ref,
              yf_ref, yb_ref, st_ref):
    L = CHUNK
    gw = SSD_HPG * SSD_P
    lane_g = lax.broadcasted_iota(jnp.int32, (L, gw), 1)

    def per_head(cols):
        out = cols[SSD_HPG - 1]
        for i in range(SSD_HPG - 2, -1, -1):
            out = jnp.where(lane_g[0:cols[0].shape[0]] < (i + 1) * SSD_P, cols[i], out)
        return out

    dirs = []
    for d, (cv_ref, gc_ref, gr_ref, out_ref) in enumerate(
            [(cvf_ref, gcf_ref, grf_ref, yf_ref), (cvb_ref, gcb_ref, grb_ref, yb_ref)]):
        rev = d == 1
        dtc = _softplus(gc_ref[...] + brow_ref[...])
        dtr = _softplus(gr_ref[...] + bcol_ref[...])
        dirs.append(dict(
            cv_ref=cv_ref, out_ref=out_ref, dtc=dtc, dtr=dtr, mask=_tri(L, rev), end=0 if rev else L - 1,
            ac_all=_dot_hi(jnp.where(_tri(L, rev), 1.0, 0.0).astype(F32), dtc * arow_ref[...]),
            ar_all=_dot_hi(dtr * acol_ref[...], jnp.where(_tri(L, not rev), 1.0, 0.0).astype(F32))))
    groups = [(d, gi) for d in range(2) for gi in range(SSD_GROUPS)]

    def stage_group(d, gi):
        cv_ref = dirs[d]['cv_ref']
        x_g = cv_ref[0, :, CV_X + gi * gw:CV_X + (gi + 1) * gw]
        b_g = cv_ref[0, :, CV_B + gi * SSD_N:CV_B + (gi + 1) * SSD_N]
        c_g = cv_ref[0, :, CV_C + gi * SSD_N:CV_C + (gi + 1) * SSD_N]
        state = st_ref[d, gi]
        return dict(x_g=x_g, state=state, cb=_dot_nt(c_g, b_g), ch=_dot(c_g, state.astype(BF16)),
                    bt=jnp.transpose(b_g.astype(F32)).astype(BF16))

    def stage_head(d, gi, hg, grp):
        dd = dirs[d]
        ln = SM_DT + d * SSD_HEADS + gi * SSD_HPG + hg
        ac = dd['ac_all'][:, ln:ln + 1]
        ar = dd['ar_all'][ln:ln + 1, :]
        last = ac[dd['end']:dd['end'] + 1, :]
        seg = jnp.exp(jnp.where(dd['mask'], ac - ar, -jnp.inf))
        return dict(mm=(seg * grp['cb'] * dd['dtr'][ln:ln + 1, :]).astype(BF16), e=jnp.exp(ac),
                    w=jnp.exp(last - ac) * dd['dtc'][:, ln:ln + 1], last=jnp.exp(last))

    grps = [stage_group(d, gi) for d, gi in groups]
    heads = [[stage_head(d, gi, hg, grps[i]) for hg in range(SSD_HPG)] for i, (d, gi) in enumerate(groups)]
    ys = [[_dot(hd['mm'], grps[i]['x_g']) for hd in heads[i]] for i in range(len(groups))]
    for i, (d, gi) in enumerate(groups):
        grp = grps[i]
        y = ys[i][SSD_HPG - 1]
        for hg in range(SSD_HPG - 2, -1, -1):
            y = jnp.where(lane_g < (hg + 1) * SSD_P, ys[i][hg], y)
        y = y + grp['ch'] * per_head([hd['e'] for hd in heads[i]])
        dirs[d]['out_ref'][0, :, gi * gw:(gi + 1) * gw] = y
        xw = (grp['x_g'].astype(F32) * per_head([hd['w'] for hd in heads[i]])).astype(BF16)
        st_ref[d, gi] = per_head([hd['last'] for hd in heads[i]]) * grp['state'] + _dot(grp['bt'], xw)


def _mlstm_kernel(*refs):
    @pl.when(pl.program_id(1) == 0)
    def _():
        refs[-2][...] = jnp.zeros_like(refs[-2])
        refs[-1][...] = jnp.zeros_like(refs[-1])

    _mlstm_body(*refs)


def _ssd_kernel(*refs):
    @pl.when(pl.program_id(1) == 0)
    def _():
        refs[-1][...] = jnp.zeros_like(refs[-1])

    _ssd_body(*refs)


def _scan_call(cvo, v, smc, smt, brow, bcol, arow, acol, n_ctx):
    bsz, s, _ = cvo.shape
    nc, nc_ctx = s // CHUNK, n_ctx // CHUNK
    qk_w = 2 * ML_QK_W

    def fwd(b, j):
        return j

    def bwd(b, j):
        return _bwd_chunk(j, nc_ctx, nc)

    def gate_specs(ch):
        return [pl.BlockSpec((CHUNK, LANES), lambda b, j: (b * nc + ch(b, j), 0)),
                pl.BlockSpec((LANES, CHUNK), lambda b, j: (0, b * nc + ch(b, j)))]

    def ml_specs(ch):
        return [pl.BlockSpec((1, CHUNK, qk_w), lambda b, j: (b, ch(b, j), CV_Q // qk_w)),
                pl.BlockSpec((1, CHUNK, ML_V_W), lambda b, j: (b, ch(b, j), 0))] + gate_specs(ch)

    def ssd_specs(ch):
        return [pl.BlockSpec((1, CHUNK, CV_SSD_W), lambda b, j: (b, ch(b, j), 0))] + gate_specs(ch)

    def const(shape):
        return pl.BlockSpec(shape, lambda b, j: (0, 0))

    def out_spec(ch, w):
        return pl.BlockSpec((1, CHUNK, w), lambda b, j: (b, ch(b, j), 0))

    h_out = jax.ShapeDtypeStruct((bsz, s, ML_V_W), F32)
    y_out = jax.ShapeDtypeStruct((bsz, s, SSD_INNER), F32)
    hf, hb = pl.pallas_call(
        _mlstm_kernel,
        grid=(bsz, nc),
        in_specs=ml_specs(fwd) + ml_specs(bwd) + [const((1, LANES)), const((LANES, 1))],
        out_specs=[out_spec(fwd, ML_V_W), out_spec(bwd, ML_V_W)],
        out_shape=[h_out, h_out],
        scratch_shapes=[pltpu.VMEM((2, ML_QK_W, ML_V + LANES), F32),
                        pltpu.VMEM((2 * ML_HEADS, 8, LANES), F32)],
        compiler_params=_params(("arbitrary", "arbitrary")),
        name="mlstm_scan",
    )(cvo, v, smc, smt, cvo, v, smc, smt, brow, bcol)
    yf, yb = pl.pallas_call(
        _ssd_kernel,
        grid=(bsz, nc),
        in_specs=ssd_specs(fwd) + ssd_specs(bwd) + [const((1, LANES)), const((LANES, 1)), const((1, LANES)),
                                                    const((LANES, 1))],
        out_specs=[out_spec(fwd, SSD_INNER), out_spec(bwd, SSD_INNER)],
        out_shape=[y_out, y_out],
        scratch_shapes=[pltpu.VMEM((2, SSD_GROUPS, SSD_N, SSD_HPG * SSD_P), F32)],
        compiler_params=_params(("arbitrary", "arbitrary")),
        name="ssd_scan",
    )(cvo, smc, smt, cvo, smc, smt, brow, bcol, arow, acol)
    return hf, hb, yf, yb


def _out_kernel(x_ref, mod_ref, att_ref, hf_ref, hb_ref, yf_ref, yb_ref, sx_ref, z_ref, o_ref, g_ref,
                mlg_ref, sd_ref, sg_ref, wa_ref, wm_ref, ws_ref, wo_ref, g2_ref, wr_ref, br_ref,
                xo_ref, h2_ref, lg_ref):
    mod = mod_ref[0, 0]
    d = D_MODEL
    gw = SSD_HPG * SSD_P
    n_sub = ROW_TILE // OUT_SUB_ROWS
    subs = [slice(r * OUT_SUB_ROWS, (r + 1) * OUT_SUB_ROWS) for r in range(n_sub)]

    def branch_inputs(rs):
        hm = hf_ref[rs, :] + hb_ref[rs, :]
        m_out = jnp.concatenate([_rms(hm[:, h * ML_V:(h + 1) * ML_V], mlg_ref[:, h * ML_V:(h + 1) * ML_V])
                                 for h in range(ML_HEADS)], axis=1) * _sigmoid(o_ref[rs, :].astype(F32))
        z = z_ref[rs, :].astype(F32)
        y = (yf_ref[rs, :] + yb_ref[rs, :] + sd_ref[...] * sx_ref[rs, :].astype(F32)) * (z * _sigmoid(z))
        s_out = jnp.concatenate([_rms(y[:, gi * gw:(gi + 1) * gw], sg_ref[:, gi * gw:(gi + 1) * gw])
                                 for gi in range(SSD_GROUPS)], axis=1)
        return m_out.astype(BF16), s_out.astype(BF16)

    ins = [branch_inputs(rs) for rs in subs]
    dots = [(_dot(att_ref[rs, :], wa_ref[...]), _dot(ins[r][0], wm_ref[...]), _dot(ins[r][1], ws_ref[...]))
            for r, rs in enumerate(subs)]
    merged = []
    for r, rs in enumerate(subs):
        gt = g_ref[rs, :].astype(F32)
        merged.append((_sigmoid(gt[:, 0:d]) * dots[r][0] + _sigmoid(gt[:, d:2 * d]) * dots[r][1]
                       + _sigmoid(gt[:, 2 * d:3 * d]) * dots[r][2]).astype(BF16))
    outs = [_dot(m, wo_ref[...]) for m in merged]
    h2s = []
    for r, rs in enumerate(subs):
        xn = x_ref[rs, :] + mod[2:3] * outs[r]
        xo_ref[rs, :] = xn
        h2 = _rms(xn, g2_ref[...]) * (1.0 + mod[4:5]) + mod[3:4]
        _to_tiles(h2_ref, h2, first_row=r * OUT_SUB_ROWS)
        h2s.append(h2)
    for r, rs in enumerate(subs):
        lg_ref[rs, :] = _dot_hi(h2s[r], wr_ref[...]) + br_ref[...]


def _out_call(x, mod, att, hf, hb, yf, yb, cvo, z, o, g, mlg, sd, sg, wa, wm, ws, wo, g2, wr, br, tiles_per_b):
    t, d = x.shape
    n_tiles = t // ROW_TILE

    def const(shape):
        return pl.BlockSpec(shape, lambda i: (0,) * len(shape), pipeline_mode=pl.Buffered(1))

    def rows(w, blk=0):
        return pl.BlockSpec((ROW_TILE, w), lambda i: (i, blk))

    return pl.pallas_call(
        _out_kernel,
        grid=(n_tiles,),
        in_specs=[rows(d),
                  pl.BlockSpec((1, 1, 6, d), lambda i: (i // tiles_per_b, jnp.minimum(i % tiles_per_b, 1), 0, 0)),
                  rows(MLA_HEADS * MLA_V), rows(ML_V_W), rows(ML_V_W), rows(SSD_INNER), rows(SSD_INNER),
                  rows(SSD_INNER, CV_X // SSD_INNER), rows(SSD_INNER), rows(ML_V_W), rows(3 * d),
                  const((1, ML_V_W)), const((1, SSD_INNER)), const((1, SSD_INNER)),
                  const(wa.shape), const(wm.shape), const(ws.shape), const(wo.shape), const((1, d)),
                  const(wr.shape), const((1, LANES))],
        out_specs=[rows(d), pl.BlockSpec((ROW_TILE * ROW_SLABS, LANES), lambda i: (i, 0)), rows(LANES)],
        out_shape=[jax.ShapeDtypeStruct((t, d), F32), jax.ShapeDtypeStruct((t * ROW_SLABS, LANES), F32),
                   jax.ShapeDtypeStruct((t, LANES), F32)],
        compiler_params=_params(("arbitrary",)),
        name="mix_out",
    )(x, mod, att, hf, hb, yf, yb, cvo, z, o, g, mlg, sd, sg, wa, wm, ws, wo, g2, wr, br)


def _route_kernel(lg_ref, e_ref, gate_ref, rank_ref, cnt_ref, base_ref):
    i = pl.program_id(0)

    @pl.when(i == 0)
    def _():
        base_ref[...] = jnp.zeros_like(base_ref)

    v = lg_ref[...]
    n = v.shape[0]
    lane = lax.broadcasted_iota(jnp.int32, v.shape, 1)
    v = jnp.where(lane < N_EXPERTS, v, -jnp.inf)
    tops, hots = [], []
    e_out = jnp.zeros(v.shape, jnp.int32)
    for k in range(TOP_K):
        mk = jnp.max(v, axis=-1, keepdims=True)
        idx = jnp.min(jnp.where(v == mk, lane, LANES), axis=-1, keepdims=True)
        hot = lane == idx
        v = jnp.where(hot, -jnp.inf, v)
        tops.append(mk)
        hots.append(hot)
        e_out = jnp.where(lane == k, idx, e_out)
    ex = [jnp.exp(tk - tops[0]) for tk in tops]
    den = ex[0]
    for k in range(1, TOP_K):
        den = den + ex[k]
    gate = jnp.zeros(v.shape, F32)
    for k in range(TOP_K):
        gate = jnp.where(lane == k, ex[k] / den, gate)
    hot_sum = jnp.zeros(v.shape, F32)
    for k in range(TOP_K):
        hot_sum = hot_sum + jnp.where(hots[k], 1.0, 0.0)
    r = lax.broadcasted_iota(jnp.int32, (n, n), 0)
    c = lax.broadcasted_iota(jnp.int32, (n, n), 1)
    before = jnp.where(c < r, 1.0, 0.0).astype(BF16)
    base = base_ref[0:1, :]
    cum = _dot(before, hot_sum.astype(BF16)) + base
    rank = jnp.zeros(v.shape, jnp.int32)
    for k in range(TOP_K):
        rk = jnp.sum(jnp.where(hots[k], cum, 0.0), axis=-1, keepdims=True)
        rank = jnp.where(lane == k, rk.astype(jnp.int32), rank)
    e_ref[...] = e_out
    gate_ref[...] = gate
    rank_ref[...] = rank
    new_base = base + jnp.sum(hot_sum, axis=0, keepdims=True)
    base_ref[...] = jnp.broadcast_to(new_base, base_ref.shape)
    cnt_ref[...] = jnp.broadcast_to(new_base, cnt_ref.shape)


def _route_call(logits):
    t = logits.shape[0]

    def rows():
        return pl.BlockSpec((ROW_TILE, LANES), lambda i: (i, 0))

    return pl.pallas_call(
        _route_kernel,
        grid=(t // ROW_TILE,),
        in_specs=[rows()],
        out_specs=[rows(), rows(), rows(), pl.BlockSpec((8, LANES), lambda i: (0, 0))],
        out_shape=[jax.ShapeDtypeStruct((t, LANES), jnp.int32), jax.ShapeDtypeStruct((t, LANES), F32),
                   jax.ShapeDtypeStruct((t, LANES), jnp.int32), jax.ShapeDtypeStruct((8, LANES), F32)],
        scratch_shapes=[pltpu.VMEM((8, LANES), F32)],
        compiler_params=_params(("arbitrary",)),
        name="moe_route",
    )(logits)


def _dest_kernel(e_ref, rank_ref, start_ref, d_ref):
    e = e_ref[...]
    rank = rank_ref[...]
    lane = lax.broadcasted_iota(jnp.int32, e.shape, 1)
    out = jnp.zeros(e.shape, jnp.int32)
    for k in range(TOP_K):
        base = jnp.sum(jnp.where(lane == e[:, k:k + 1], start_ref[...], 0), axis=-1, keepdims=True)
        out = jnp.where(lane == k, (base + rank[:, k:k + 1]) * SUBLANES, out)
    d_ref[...] = out


def _dest_call(e_pad, rank_pad, start_row):
    t = e_pad.shape[0]

    def rows():
        return pl.BlockSpec((ROW_TILE, LANES), lambda i: (i, 0))

    return pl.pallas_call(
        _dest_kernel,
        grid=(t // ROW_TILE,),
        in_specs=[rows(), rows(), pl.BlockSpec((1, LANES), lambda i: (0, 0))],
        out_specs=rows(),
        out_shape=jax.ShapeDtypeStruct((t, LANES), jnp.int32),
        compiler_params=_params(("arbitrary",)),
        name="moe_dest",
    )(e_pad, rank_pad, start_row)


def _dispatch_kernel(lo_ref, hi_ref, dst_ref, h_ref, xs_hbm, stage, zrow, sem, zsem):
    i = pl.program_id(0)
    n = pl.num_programs(0)
    slot = i % 2
    stage[slot] = h_ref[...]

    def tile(ref, first):
        return ref.at[pl.ds(pl.multiple_of(first, SUBLANES), SUBLANES), :]

    def zero_copy(r):
        return pltpu.make_async_copy(zrow, tile(xs_hbm, r * SUBLANES), zsem)

    @pl.when(i == 0)
    def _():
        zrow[...] = jnp.zeros_like(zrow)
        for e in range(N_EXPERTS + 1):
            def start(r, carry):
                zero_copy(r).start()
                return carry
            lax.fori_loop(lo_ref[e], hi_ref[e], start, 0)
        for e in range(N_EXPERTS + 1):
            def wait(r, carry):
                zero_copy(r).wait()
                return carry
            lax.fori_loop(lo_ref[e], hi_ref[e], wait, 0)

    def body(r, carry):
        src = tile(stage.at[slot], r * SUBLANES)
        for k in range(TOP_K):
            pltpu.make_async_copy(src, tile(xs_hbm, dst_ref[0, 0, r * TOP_K + k]), sem.at[slot]).start()
        return carry
    lax.fori_loop(0, ROW_TILE, body, 0, unroll=DMA_ISSUE_UNROLL)

    def wait_step(s):
        for _ in range(TOP_K):
            pltpu.make_async_copy(stage.at[s], xs_hbm.at[pl.ds(0, ROW_TILE * SUBLANES), :], sem.at[s]).wait()

    @pl.when(i > 0)
    def _():
        wait_step(1 - slot)

    @pl.when(i == n - 1)
    def _():
        wait_step(slot)


def _dispatch_call(fill_lo, fill_hi, dst3, h2, n_rows):
    n_tiles = dst3.shape[0]
    grid_spec = pltpu.PrefetchScalarGridSpec(
        num_scalar_prefetch=2,
        grid=(n_tiles,),
        in_specs=[pl.BlockSpec((1, 1, ROW_TILE * TOP_K), lambda i, lo, hi: (i, 0, 0), memory_space=pltpu.SMEM),
                  pl.BlockSpec((ROW_TILE * ROW_SLABS, LANES), lambda i, lo, hi: (i, 0))],
        out_specs=pl.BlockSpec(memory_space=pl.ANY),
        scratch_shapes=[pltpu.VMEM((2, ROW_TILE * ROW_SLABS, LANES), F32), pltpu.VMEM((SUBLANES, LANES), F32),
                        pltpu.SemaphoreType.DMA((2,)), pltpu.SemaphoreType.DMA(())],
    )
    return pl.pallas_call(
        _dispatch_kernel,
        grid_spec=grid_spec,
        out_shape=jax.ShapeDtypeStruct((n_rows * ROW_SLABS, LANES), F32),
        compiler_params=_params(("arbitrary",)),
        name="moe_dispatch",
    )(fill_lo, fill_hi, dst3, h2)


def _moe_kernel(be_ref, nb_ref, x_ref, wu_ref, bu_ref, wd_ref, bd_ref, y_ref, wub, wdb):
    i = pl.program_id(0)

    @pl.when(i < nb_ref[0])
    def _():
        prev = be_ref[jnp.maximum(i - 1, 0)]

        @pl.when((i == 0) | (be_ref[i] != prev))
        def _():
            wub[...] = wu_ref[...].astype(BF16)
            wdb[...] = wd_ref[...].astype(BF16)

        gu = _dot(_from_tiles(x_ref, MOE_ROWS).astype(BF16), wub[...]) + bu_ref[...]
        glu = jnp.minimum(gu[:, :D_FF], SWIGLU_LIMIT)
        lin = jnp.clip(gu[:, D_FF:], -SWIGLU_LIMIT, SWIGLU_LIMIT)
        act = glu * _sigmoid(SWIGLU_ALPHA * glu) * (lin + 1.0)
        _to_tiles(y_ref, _dot(act.astype(BF16), wdb[...]) + bd_ref[...])

    @pl.when(i >= nb_ref[0])
    def _():
        y_ref[...] = jnp.zeros_like(y_ref)


def _moe_call(block_e, n_used, xs, w_up, b_up, w_down, b_down, layer):
    n_blocks = block_e.shape[0]
    d = D_MODEL
    blk_rows = MOE_ROWS * ROW_SLABS
    b_up = b_up.reshape(b_up.shape[0], N_EXPERTS, 1, 2 * D_FF)
    b_down = b_down.reshape(b_down.shape[0], N_EXPERTS, 1, d)

    def blk(i, nb):
        return jnp.minimum(i, nb[0] - 1)

    def wspec(shape):
        return pl.BlockSpec((None, None) + shape, lambda i, be, nb: (layer, be[blk(i, nb)], 0, 0))

    grid_spec = pltpu.PrefetchScalarGridSpec(
        num_scalar_prefetch=2,
        grid=(n_blocks,),
        in_specs=[pl.BlockSpec((blk_rows, LANES), lambda i, be, nb: (blk(i, nb), 0)),
                  wspec((d, 2 * D_FF)), wspec((1, 2 * D_FF)), wspec((D_FF, d)), wspec((1, d))],
        out_specs=pl.BlockSpec((blk_rows, LANES), lambda i, be, nb: (i, 0)),
        scratch_shapes=[pltpu.VMEM((d, 2 * D_FF), BF16), pltpu.VMEM((D_FF, d), BF16)],
    )
    return pl.pallas_call(
        _moe_kernel,
        grid_spec=grid_spec,
        out_shape=jax.ShapeDtypeStruct((n_blocks * blk_rows, LANES), F32),
        compiler_params=_params(("arbitrary",)),
        name="moe_experts",
    )(block_e, n_used, xs, w_up, b_up, w_down, b_down)


def _combine_kernel(dst_ref, dstn_ref, x_ref, mod_ref, gate_ref, y_hbm, o_ref, ybuf, sem):
    i = pl.program_id(0)
    n = pl.num_programs(0)
    slot = i % 2

    def gather(dref, dst_slot):
        def body(r, carry):
            for k in range(TOP_K):
                first = pl.multiple_of(dref[0, 0, r * TOP_K + k], SUBLANES)
                pltpu.make_async_copy(y_hbm.at[pl.ds(first, SUBLANES), :],
                                      ybuf.at[dst_slot, k, pl.ds(pl.multiple_of(r * SUBLANES, SUBLANES), SUBLANES), :],
                                      sem.at[dst_slot]).start()
            return carry
        lax.fori_loop(0, ROW_TILE, body, 0, unroll=DMA_ISSUE_UNROLL)

    @pl.when(i == 0)
    def _():
        gather(dst_ref, 0)

    @pl.when(i + 1 < n)
    def _():
        gather(dstn_ref, 1 - slot)

    for k in range(TOP_K):
        pltpu.make_async_copy(y_hbm.at[pl.ds(0, ROW_TILE * SUBLANES), :], ybuf.at[slot, k], sem.at[slot]).wait()
    gate = gate_ref[...]
    f = gate[:, 0:1] * _from_tiles(ybuf.at[slot, 0], ROW_TILE)
    for k in range(1, TOP_K):
        f = f + gate[:, k:k + 1] * _from_tiles(ybuf.at[slot, k], ROW_TILE)
    o_ref[...] = x_ref[...] + mod_ref[0, 0][5:6] * f


def _combine_call(dst3, x, mod, gate, y, tiles_per_b):
    t, d = x.shape
    n_tiles = t // ROW_TILE
    return pl.pallas_call(
        _combine_kernel,
        grid=(n_tiles,),
        in_specs=[pl.BlockSpec((1, 1, ROW_TILE * TOP_K), lambda i: (i, 0, 0), memory_space=pltpu.SMEM),
                  pl.BlockSpec((1, 1, ROW_TILE * TOP_K), lambda i: (jnp.minimum(i + 1, n_tiles - 1), 0, 0),
                               memory_space=pltpu.SMEM),
                  pl.BlockSpec((ROW_TILE, d), lambda i: (i, 0)),
                  pl.BlockSpec((1, 1, 6, d), lambda i: (i // tiles_per_b, jnp.minimum(i % tiles_per_b, 1), 0, 0)),
                  pl.BlockSpec((ROW_TILE, LANES), lambda i: (i, 0)),
                  pl.BlockSpec(memory_space=pl.ANY)],
        out_specs=pl.BlockSpec((ROW_TILE, d), lambda i: (i, 0)),
        out_shape=jax.ShapeDtypeStruct((t, d), F32),
        scratch_shapes=[pltpu.VMEM((2, TOP_K, ROW_TILE * ROW_SLABS, LANES), F32),
                        pltpu.SemaphoreType.DMA((2,))],
        compiler_params=_params(("arbitrary",)),
        name="moe_combine",
    )(dst3, dst3, x, mod, gate, y)


def _final_kernel(x_ref, g_ref, o_ref):
    o_ref[0] = _rms(x_ref[0], g_ref[...])


def _final_call(x3, g, n_ctx):
    bsz, s, d = x3.shape
    skip = n_ctx // ROW_TILE
    return pl.pallas_call(
        _final_kernel,
        grid=(bsz, (s - n_ctx) // ROW_TILE),
        in_specs=[pl.BlockSpec((1, ROW_TILE, d), lambda b, i: (b, i + skip, 0)),
                  pl.BlockSpec((1, d), lambda b, i: (0, 0))],
        out_specs=pl.BlockSpec((1, ROW_TILE, d), lambda b, i: (b, i, 0)),
        out_shape=jax.ShapeDtypeStruct((bsz, s - n_ctx, d), F32),
        compiler_params=_params(("arbitrary", "arbitrary")),
        name="final_norm",
    )(x3, g)


def _rot_cols(w):
    half = MLA_ROPE // 2
    return jnp.concatenate([-w[..., half:], w[..., :half]], axis=-1)


def _pack_weights(w_in, mla_w_uq, mla_w_ukv):
    depth = w_in.shape[0]
    sizes = (MLA_Q_LORA, MLA_KV_LORA, MLA_ROPE, 2 * ML_QK_W, ML_V_W, ML_V_W, 4 * ML_HEADS, SSD_INNER,
             SSD_INNER + 2 * SSD_BC_W, 2 * SSD_HEADS, 3 * D_MODEL)
    idx = [int(v) for v in np.cumsum(sizes)[:-1]]
    w_q, w_kv, w_kr, w_qk, w_v, w_o, w_if, w_z, w_xbc, w_dt, w_g = jnp.split(w_in, idx, axis=-1)
    pad = jnp.zeros(w_in.shape[:2] + (LANES - SM_DT - 2 * SSD_HEADS,), w_in.dtype)
    small = jnp.concatenate([w_kr, _rot_cols(w_kr), w_if, w_dt, pad], axis=-1)
    wp = jnp.concatenate([w_q, w_kv, small, w_xbc, w_qk, w_v, w_o, w_z, w_g], axis=-1).astype(BF16)
    wst = jnp.swapaxes(small, 1, 2).astype(BF16)

    wq = mla_w_uq.reshape(depth, MLA_Q_LORA, MLA_HEADS, MLA_NOPE + MLA_ROPE)
    nope, rope = wq[..., :MLA_NOPE], wq[..., MLA_NOPE:]
    zr = jnp.zeros_like(rope)
    zn = jnp.zeros_like(nope)
    wqm = jnp.concatenate([nope, rope, zr], axis=-1).reshape(depth, MLA_Q_LORA, -1).astype(BF16)
    wqr = jnp.concatenate([zn, _rot_cols(rope), zr], axis=-1).reshape(depth, MLA_Q_LORA, -1).astype(BF16)
    wkv = mla_w_ukv.reshape(depth, MLA_KV_LORA, MLA_HEADS, MLA_NOPE + MLA_V)
    knope, val = wkv[..., :MLA_NOPE], wkv[..., MLA_NOPE:]
    wk = jnp.concatenate([knope, jnp.zeros_like(knope)], axis=-1).reshape(depth, MLA_KV_LORA, -1).astype(BF16)
    zv = jnp.zeros_like(val)
    val_even = jnp.concatenate([val, zv], axis=-1)[:, :, 0::2]
    val_odd = jnp.concatenate([zv, val], axis=-1)[:, :, 1::2]
    wv = jnp.stack([val_even, val_odd], axis=3).reshape(depth, MLA_KV_LORA, -1).astype(BF16)
    return wp, wst, wqm, wqr, wk, wv


def _value_ones_row():
    e = np.zeros((1, MLA_HEADS * HEAD_PAD), np.float32)
    for h in range(MLA_HEADS):
        e[0, h * HEAD_PAD + V_ONES_LANE[h % 2]] = 1.0
    return jnp.asarray(e)


def _place_matrix():
    e = np.zeros((LANES, MLA_HEADS * HEAD_PAD), np.float32)
    for h in range(MLA_HEADS):
        for jj in range(MLA_ROPE):
            e[SM_KR + jj, h * HEAD_PAD + MLA_NOPE + jj] = 1.0
            e[SM_KROT + jj, h * HEAD_PAD + MLA_NOPE + jj] = 1.0
    return jnp.asarray(e, BF16)


def _rope_tables(n_ctx, n_lat):
    rows = n_lat // GRID_W
    row = jnp.broadcast_to(jnp.arange(rows)[:, None], (rows, GRID_W)).reshape(-1)
    col = jnp.broadcast_to(jnp.arange(GRID_W)[None, :], (rows, GRID_W)).reshape(-1)
    n_freq = MLA_ROPE // 4
    inv = ROPE_BASE ** (-jnp.arange(n_freq, dtype=F32) / n_freq)
    ang = jnp.concatenate([row[:, None] * inv, col[:, None] * inv], axis=-1)
    cos = jnp.concatenate([jnp.ones((n_ctx, MLA_ROPE // 2), F32), jnp.cos(ang)], axis=0)
    sin = jnp.concatenate([jnp.zeros((n_ctx, MLA_ROPE // 2), F32), jnp.sin(ang)], axis=0)
    cs32 = jnp.concatenate([cos, cos], axis=-1)
    sn32 = jnp.concatenate([sin, sin], axis=-1)
    s = n_ctx + n_lat
    z32 = jnp.zeros((s, MLA_ROPE), F32)
    t1 = jnp.concatenate([cs32, sn32, jnp.zeros((s, LANES - 2 * MLA_ROPE), F32)], axis=-1)
    cs = jnp.concatenate([jnp.ones((s, MLA_NOPE), F32), cs32, z32], axis=-1)
    sn = jnp.concatenate([jnp.zeros((s, MLA_NOPE), F32), sn32, z32], axis=-1)
    return t1, cs, sn


def _small_lanes(if_vals, dt_vals):
    v = jnp.zeros((LANES,), F32)
    v = v.at[SM_IF:SM_IF + if_vals.shape[0]].set(if_vals)
    return v.at[SM_DT:SM_DT + dt_vals.shape[0]].set(dt_vals)


def kernel(x, c, ctx, c_ctx, w_ada, b_ada, norm1_g, w_in, mla_qnorm_g, mla_w_uq, mla_kvnorm_g, mla_w_ukv,
           ml_conv_w, ml_conv_b, ml_gate_b, ml_norm_g, ssd_conv_w, ssd_conv_b, ssd_dt_bias, ssd_a_log, ssd_d,
           ssd_norm_g, w_br_mla, w_br_ml, w_br_ssd, w_out, norm2_g, w_router, b_router, w_up, b_up, w_down,
           b_down, final_g):
    bsz, n_lat, d = x.shape
    n_ctx = ctx.shape[1]
    depth = w_in.shape[0]
    s = n_ctx + n_lat
    t = bsz * s
    tiles_per_b = s // ROW_TILE
    assert n_ctx == ROW_TILE and n_lat % ROW_TILE == 0 and d == D_MODEL

    cond = jnp.zeros((16, d), F32).at[:bsz].set(c).at[bsz].set(c_ctx)
    mod_all = _ada_call(cond, w_ada, b_ada)
    mod_lat = mod_all[:, :bsz].reshape(depth, bsz, 1, 6, d)
    mod_ctx = jnp.broadcast_to(mod_all[:, bsz].reshape(depth, 1, 1, 6, d), (depth, bsz, 1, 6, d))
    mod_tab = jnp.concatenate([mod_ctx, mod_lat], axis=2)

    wp, wst, wqm, wqr, wk, wv = _pack_weights(w_in, mla_w_uq, mla_w_ukv)
    emat = _place_matrix()
    vone = _value_ones_row()
    t1, cs, sn = _rope_tables(n_ctx, n_lat)
    conv_w = jnp.concatenate([ssd_conv_w, ml_conv_w], axis=-1)
    conv_w = jnp.concatenate([conv_w, jnp.zeros((depth, 8 - CONV_W, CV_W), F32)], axis=1)
    conv_b = jnp.concatenate([ssd_conv_b, ml_conv_b], axis=-1).reshape(depth, 1, CV_W)
    post = jnp.ones((1, CV_W), F32).at[:, CV_K:CV_K + ML_QK_W].set(ML_QK ** -0.5)
    wr = jnp.concatenate([w_router, jnp.zeros((depth, d, LANES - N_EXPERTS), F32)], axis=-1)
    br = jnp.concatenate([b_router, jnp.zeros((depth, LANES - N_EXPERTS), F32)], axis=-1)

    xall = jnp.concatenate([ctx, x], axis=1).reshape(t, d)
    n_rows_max = -(-(t * TOP_K + N_EXPERTS * (MOE_ROWS - 1)) // MOE_ROWS) * MOE_ROWS
    n_blocks = n_rows_max // MOE_ROWS

    for l in range(depth):
        q, k, vv, smc, smt, cv, v, o, z, g = _in_call(
            xall, mod_tab[l], norm1_g[l].reshape(1, d), wp[l], wst[l], mla_qnorm_g[l].reshape(1, -1), wqm[l],
            wqr[l], mla_kvnorm_g[l].reshape(1, -1), wk[l], wv[l], vone, emat, t1, cs, sn, tiles_per_b)
        cvo = _conv_call(cv.reshape(bsz, s, CV_W), conv_w[l], conv_b[l], post, n_ctx)
        att = _attn_call(q.reshape(bsz, s, -1), k.reshape(bsz, s, -1), vv.reshape(bsz, s, -1), n_ctx)
        bias = _small_lanes(ml_gate_b[l].reshape(-1), ssd_dt_bias[l].reshape(-1))
        a_vec = _small_lanes(jnp.zeros((4 * ML_HEADS,), F32), -jnp.exp(ssd_a_log[l].astype(F32)).reshape(-1))
        brow, bcol = bias.reshape(1, LANES), bias.reshape(LANES, 1)
        hf, hb, yf, yb = _scan_call(cvo, v.reshape(bsz, s, -1), smc, smt, brow, bcol, a_vec.reshape(1, LANES),
                                    a_vec.reshape(LANES, 1), n_ctx)
        xall, h2, logits = _out_call(
            xall, mod_tab[l], att.reshape(t, -1), hf.reshape(t, -1), hb.reshape(t, -1), yf.reshape(t, -1),
            yb.reshape(t, -1), cvo.reshape(t, CV_W), z, o, g, ml_norm_g[l].reshape(1, -1),
            jnp.repeat(ssd_d[l], SSD_P).reshape(1, -1), ssd_norm_g[l].reshape(1, -1),
            w_br_mla[l].astype(BF16), w_br_ml[l].astype(BF16), w_br_ssd[l].astype(BF16), w_out[l].astype(BF16),
            norm2_g[l].reshape(1, d), wr[l], br[l].reshape(1, LANES), tiles_per_b)

        e_pad, gate, rank_pad, cnt = _route_call(logits)
        counts = cnt[0, :N_EXPERTS].astype(jnp.int32)
        padded = (counts + MOE_ROWS - 1) // MOE_ROWS * MOE_ROWS
        pad_end = jnp.cumsum(padded)
        pad_start = pad_end - padded
        start_row = jnp.zeros((1, LANES), jnp.int32).at[0, :N_EXPERTS].set(pad_start)
        block_first = jnp.arange(n_blocks, dtype=jnp.int32) * MOE_ROWS
        block_e = jnp.minimum(jnp.sum(pad_end[None, :] <= block_first[:, None], axis=1), N_EXPERTS - 1)
        n_used = (pad_end[-1] // MOE_ROWS).astype(jnp.int32).reshape(1)
        dest = _dest_call(e_pad, rank_pad, start_row)
        dst3 = dest[:, :TOP_K].reshape(t // ROW_TILE, 1, ROW_TILE * TOP_K)
        fill_lo = jnp.concatenate([pad_start + counts, pad_end[-1:]])
        fill_hi = jnp.concatenate([pad_end, jnp.full((1,), n_rows_max, jnp.int32)])
        xs = _dispatch_call(fill_lo, fill_hi, dst3, h2, n_rows_max)
        y = _moe_call(block_e.astype(jnp.int32), n_used, xs, w_up, b_up, w_down, b_down, l)
        xall = _combine_call(dst3, xall, mod_tab[l], gate, y, tiles_per_b)

    return _final_call(xall.reshape(bsz, s, d), final_g.reshape(1, d), n_ctx)
```
